```python
import jax, jax.numpy as jnp
from jax import lax
import numpy as np

D_MODEL = 1024
BATCH = 32
SEQ = 2048
DEPTH = 1
DEC_BATCH = 16
DEC_SEQ = 2048
PAST_LEN = 128

EPS = 1e-6
GLA_HEADS = 4
GLA_DK = 128
GLA_DV = 256
GLA_QK = GLA_HEADS * GLA_DK
GLA_VW = GLA_HEADS * GLA_DV
GLA_RANK = 16
GLA_TEMP = 16.0
GLA_CHUNK = 64
MLA_HEADS = 8
MLA_Q_RANK = 256
MLA_KV_RANK = 256
MLA_NOPE = 128
MLA_ROPE = 64
MLA_V = 128
MLA_VW = MLA_HEADS * MLA_V
ROPE_BASE = 10000.0
Q_BLOCK = 128
D_FF = ((8 * D_MODEL + 3 * 256 - 1) // (3 * 256)) * 256
IN_WIDTHS = (GLA_QK, GLA_QK, GLA_VW, GLA_VW, GLA_RANK, GLA_RANK,
             MLA_Q_RANK, MLA_KV_RANK, MLA_ROPE, D_MODEL, D_MODEL)
D_IN = 2 * GLA_QK + 2 * GLA_VW + 2 * GLA_RANK + MLA_Q_RANK + MLA_KV_RANK + MLA_ROPE + 2 * D_MODEL

kernel_name = "hybrid_gla_mla_gated_encoder"


def rms_norm(x, g):
    xf = x.astype(jnp.float32)
    y = xf * lax.rsqrt(jnp.mean(xf * xf, axis=-1, keepdims=True) + EPS)
    return (y * g.astype(jnp.float32)).astype(x.dtype)


def rope_tables(L):
    inv = ROPE_BASE ** (-jnp.arange(0, MLA_ROPE, 2, dtype=jnp.float32) / MLA_ROPE)
    ang = jnp.arange(L, dtype=jnp.float32)[:, None] * inv[None, :]
    return jnp.cos(ang), jnp.sin(ang)


def apply_rope(x, cos, sin):
    x1, x2 = jnp.split(x.astype(jnp.float32), 2, axis=-1)
    return jnp.concatenate([x1 * cos - x2 * sin, x1 * sin + x2 * cos], axis=-1).astype(x.dtype)


def gla_scan(q, k, v, log_a):
    B, L = q.shape[0], q.shape[1]
    nc = L // GLA_CHUNK

    def chunks(t):
        return t.astype(jnp.float32).reshape(B, nc, GLA_CHUNK, GLA_HEADS, t.shape[-1]).transpose(1, 0, 3, 2, 4)

    qc, kc, vc, ac = chunks(q), chunks(k), chunks(v), chunks(log_a)
    b = jnp.cumsum(ac, axis=3)
    b_end = b[:, :, :, -1:, :]
    q_dec = qc * jnp.exp(b)
    k_inv = kc * jnp.exp(-b)
    k_end = kc * jnp.exp(b_end - b)
    mask = jnp.tril(jnp.ones((GLA_CHUNK, GLA_CHUNK), dtype=bool))
    scores = jnp.einsum('nbhid,nbhjd->nbhij', q_dec, k_inv)
    o_intra = jnp.einsum('nbhij,nbhje->nbhie', jnp.where(mask, scores, 0.0), vc)

    def step(S, inp):
        q_n, k_n, v_n, dec_n = inp
        o_n = jnp.einsum('bhid,bhde->bhie', q_n, S)
        S = dec_n[..., None] * S + jnp.einsum('bhjd,bhje->bhde', k_n, v_n)
        return S, o_n

    S0 = jnp.zeros((B, GLA_HEADS, GLA_DK, GLA_DV), jnp.float32)
    _, o_inter = lax.scan(step, S0, (q_dec, k_end, vc, jnp.exp(b_end[:, :, :, 0, :])))
    o = o_intra + o_inter
    return o.transpose(1, 0, 3, 2, 4).reshape(B, L, GLA_HEADS, GLA_DV)


def gla_branch(q_raw, k_raw, v_raw, g_raw, af_raw, ab_raw, p):
    B, L = q_raw.shape[0], q_raw.shape[1]
    q = q_raw.reshape(B, L, GLA_HEADS, GLA_DK) * (GLA_DK ** -0.5)
    k = k_raw.reshape(B, L, GLA_HEADS, GLA_DK)
    v = v_raw.reshape(B, L, GLA_HEADS, GLA_DV)
    z_f = af_raw @ p['gla_wa_fwd'] + p['gla_ba_fwd']
    z_b = ab_raw @ p['gla_wa_bwd'] + p['gla_ba_bwd']
    log_a_f = (jax.nn.log_sigmoid(z_f.astype(jnp.float32)) / GLA_TEMP).reshape(B, L, GLA_HEADS, GLA_DK)
    log_a_b = (jax.nn.log_sigmoid(z_b.astype(jnp.float32)) / GLA_TEMP).reshape(B, L, GLA_HEADS, GLA_DK)
    o_f = gla_scan(q, k, v, log_a_f)
    o_b = gla_scan(q[:, ::-1], k[:, ::-1], v[:, ::-1], log_a_b[:, ::-1])[:, ::-1]
    o = rms_norm(o_f + o_b, p['gla_norm']).astype(q_raw.dtype)
    o = o.reshape(B, L, GLA_VW) * jax.nn.silu(g_raw)
    return o @ p['w_o_gla']


def mla_branch(cq_raw, ckv_raw, kr_raw, p):
    B, L = cq_raw.shape[0], cq_raw.shape[1]
    cos, sin = rope_tables(L)
    c_q = rms_norm(cq_raw, p['mla_norm_q'])
    q = (c_q @ p['w_uq']).reshape(B, L, MLA_HEADS, MLA_NOPE + MLA_ROPE)
    q_nope, q_pe = q[..., :MLA_NOPE], q[..., MLA_NOPE:]
    q_pe = apply_rope(q_pe, cos[:, None, :], sin[:, None, :])
    c_kv = rms_norm(ckv_raw, p['mla_norm_kv'])
    k_nope = (c_kv @ p['w_uk']).reshape(B, L, MLA_HEADS, MLA_NOPE)
    v = (c_kv @ p['w_uv']).reshape(B, L, MLA_HEADS, MLA_V)
    k_pe = apply_rope(kr_raw, cos, sin)
    scale = (MLA_NOPE + MLA_ROPE) ** -0.5
    nb = L // Q_BLOCK
    qn_b = q_nope.reshape(B, nb, Q_BLOCK, MLA_HEADS, MLA_NOPE).transpose(1, 0, 2, 3, 4)
    qp_b = q_pe.reshape(B, nb, Q_BLOCK, MLA_HEADS, MLA_ROPE).transpose(1, 0, 2, 3, 4)

    def attend(blk):
        qn, qp = blk
        s = jnp.einsum('bqhd,bkhd->bhqk', qn, k_nope) + jnp.einsum('bqhr,bkr->bhqk', qp, k_pe)
        pr = jax.nn.softmax(s.astype(jnp.float32) * scale, axis=-1).astype(v.dtype)
        return jnp.einsum('bhqk,bkhe->bqhe', pr, v)

    o = lax.map(attend, (qn_b, qp_b))
    o = o.transpose(1, 0, 2, 3, 4).reshape(B, L, MLA_VW)
    return o @ p['w_o_mla']


def encoder_layer(x, p):
    h = rms_norm(x, p['norm_mix_pre'])
    split_idx = np.cumsum(IN_WIDTHS)[:-1].tolist()
    (q_raw, k_raw, v_raw, g_raw, af_raw, ab_raw,
     cq_raw, ckv_raw, kr_raw, gate_a, gate_b) = jnp.split(h @ p['w_in'], split_idx, axis=-1)
    y_a = gla_branch(q_raw, k_raw, v_raw, g_raw, af_raw, ab_raw, p)
    y_b = mla_branch(cq_raw, ckv_raw, kr_raw, p)
    merged = jax.nn.sigmoid(gate_a) * y_a + jax.nn.sigmoid(gate_b) * y_b
    x = x + rms_norm(merged @ p['w_out'], p['norm_mix_post'])
    h = rms_norm(x, p['norm_ffn_pre'])
    f = (jax.nn.silu(h @ p['w_gate']) * (h @ p['w_up'])) @ p['w_down']
    return x + rms_norm(f, p['norm_ffn_post'])


def _w(key, shape, fan_in):
    return jax.random.normal(key, shape, jnp.float32) * (fan_in ** -0.5)


def _gain(key, shape):
    return 1.0 + 0.02 * jax.random.normal(key, shape, jnp.float32)


def _bias(key, shape):
    return 0.1 * jax.random.normal(key, shape, jnp.float32)


def setup_inputs(seed: int = 0) -> dict:
    key = jax.random.key(seed)
    ks = jax.random.split(key, 24)
    L = DEPTH
    return {
        'x_prompt': jax.random.normal(ks[0], (BATCH, SEQ, D_MODEL), jnp.float32),
        'x_sample': jax.random.normal(ks[1], (DEC_BATCH, DEC_SEQ, D_MODEL), jnp.float32),
        'norm_mix_pre': _gain(ks[2], (L, D_MODEL)),
        'w_in': _w(ks[3], (L, D_MODEL, D_IN), D_MODEL),
        'gla_wa_fwd': _w(ks[4], (L, GLA_RANK, GLA_QK), GLA_RANK),
        'gla_ba_fwd': _bias(ks[5], (L, GLA_QK)),
        'gla_wa_bwd': _w(ks[6], (L, GLA_RANK, GLA_QK), GLA_RANK),
        'gla_ba_bwd': _bias(ks[7], (L, GLA_QK)),
        'gla_norm': _gain(ks[8], (L, GLA_DV)),
        'w_o_gla': _w(ks[9], (L, GLA_VW, D_MODEL), GLA_VW),
        'mla_norm_q': _gain(ks[10], (L, MLA_Q_RANK)),
        'w_uq': _w(ks[11], (L, MLA_Q_RANK, MLA_HEADS * (MLA_NOPE + MLA_ROPE)), MLA_Q_RANK),
        'mla_norm_kv': _gain(ks[12], (L, MLA_KV_RANK)),
        'w_uk': _w(ks[13], (L, MLA_KV_RANK, MLA_HEADS * MLA_NOPE), MLA_KV_RANK),
        'w_uv': _w(ks[14], (L, MLA_KV_RANK, MLA_HEADS * MLA_V), MLA_KV_RANK),
        'w_o_mla': _w(ks[15], (L, MLA_VW, D_MODEL), MLA_VW),
        'w_out': _w(ks[16], (L, D_MODEL, D_MODEL), D_MODEL),
        'norm_mix_post': _gain(ks[17], (L, D_MODEL)),
        'norm_ffn_pre': _gain(ks[18], (L, D_MODEL)),
        'w_gate': _w(ks[19], (L, D_MODEL, D_FF), D_MODEL),
        'w_up': _w(ks[20], (L, D_MODEL, D_FF), D_MODEL),
        'w_down': _w(ks[21], (L, D_FF, D_MODEL), D_FF),
        'norm_ffn_post': _gain(ks[22], (L, D_MODEL)),
    }


def reference(x_prompt, x_sample, norm_mix_pre, w_in, gla_wa_fwd, gla_ba_fwd, gla_wa_bwd,
              gla_ba_bwd, gla_norm, w_o_gla, mla_norm_q, w_uq, mla_norm_kv, w_uk, w_uv,
              w_o_mla, w_out, norm_mix_post, norm_ffn_pre, w_gate, w_up, w_down, norm_ffn_post):
    params = dict(norm_mix_pre=norm_mix_pre, w_in=w_in, gla_wa_fwd=gla_wa_fwd,
                  gla_ba_fwd=gla_ba_fwd, gla_wa_bwd=gla_wa_bwd, gla_ba_bwd=gla_ba_bwd,
                  gla_norm=gla_norm, w_o_gla=w_o_gla, mla_norm_q=mla_norm_q, w_uq=w_uq,
                  mla_norm_kv=mla_norm_kv, w_uk=w_uk, w_uv=w_uv, w_o_mla=w_o_mla,
                  w_out=w_out, norm_mix_post=norm_mix_post, norm_ffn_pre=norm_ffn_pre,
                  w_gate=w_gate, w_up=w_up, w_down=w_down, norm_ffn_post=norm_ffn_post)

    def trunk(x):
        for layer in range(DEPTH):
            x = encoder_layer(x, {name: arr[layer] for name, arr in params.items()})
        return x

    y_prompt = trunk(x_prompt)
    y_sample = trunk(x_sample)
    return (y_prompt, y_sample)
```

```python
import functools

import jax
import jax.numpy as jnp
import numpy as np
from jax import lax
from jax.experimental import pallas as pl
from jax.experimental.pallas import tpu as pltpu

F32 = jnp.float32
BF16 = jnp.bfloat16

EPS = 1e-6
GLA_HEADS = 4
GLA_DK = 128
GLA_DV = 256
GLA_RANK = 16
GLA_TEMP = 16.0
GLA_CHUNK = 128
MLA_HEADS = 8
MLA_NOPE = 128
MLA_ROPE = 64
MLA_V = 128
ROPE_BASE = 10000.0

IN_TOKENS = 512
OUT_TOKENS = 256
Q_TOKENS = 256
NORM_ROWS = 256
VMEM_LIMIT_BYTES = 56 * 1024 * 1024


def _rms(xf, g):
    ms = jnp.mean(xf * xf, axis=-1, keepdims=True)
    return xf * lax.rsqrt(ms + EPS) * g


def _dot(a, b):
    return jnp.dot(a, b, preferred_element_type=F32)


def _dot_nt(a, b):
    return lax.dot_general(a, b, (((1,), (1,)), ((), ())), preferred_element_type=F32)


def _params():
    return pltpu.CompilerParams(
        dimension_semantics=("arbitrary", "arbitrary"),
        vmem_limit_bytes=VMEM_LIMIT_BYTES,
    )


def _resident(shape):
    nd = len(shape)
    return pl.BlockSpec(shape, lambda *_: (0,) * nd, pipeline_mode=pl.Buffered(1))


def _in_proj_kernel(x_ref, g_ref, cc_ref, ss_ref, nq_ref, nkv_ref,
                    wq_ref, wk_ref, wv_ref, wg_ref, wlat_ref, wmisc_ref, wga_ref, wgb_ref,
                    q_out, k_out, v_out, g_out, dec_out, cq_out, ckv_out, kpe_out,
                    ga_out, gb_out):
    h = _rms(x_ref[0], g_ref[...]).astype(BF16)

    q_out[0] = _dot(h, wq_ref[...]).astype(BF16)
    k_out[0] = _dot(h, wk_ref[...]).astype(BF16)
    v_out[0] = _dot(h, wv_ref[...]).astype(BF16)
    g_out[0] = _dot(h, wg_ref[...]).astype(BF16)

    lat = _dot(h, wlat_ref[...])
    r = nq_ref.shape[1]
    cq_out[0] = _rms(lat[:, :r], nq_ref[...]).astype(BF16)
    ckv_out[0] = _rms(lat[:, r:], nkv_ref[...]).astype(BF16)

    misc = _dot(h, wmisc_ref[...])
    rot = misc[:, :2 * MLA_ROPE] * jnp.concatenate([cc_ref[...], ss_ref[...]], axis=1)
    rot = rot + pltpu.roll(rot, MLA_ROPE, axis=1)
    kpe_out[0] = rot[:, :MLA_ROPE].astype(BF16)
    dec_out[0] = misc[:, 2 * MLA_ROPE:2 * MLA_ROPE + 2 * GLA_RANK]

    ga_out[0] = jax.nn.sigmoid(_dot(h, wga_ref[...])).astype(BF16)
    gb_out[0] = jax.nn.sigmoid(_dot(h, wgb_ref[...])).astype(BF16)


def _in_proj(x, p):
    B, L, D = x.shape
    tm = min(IN_TOKENS, L)
    grid = (B, L // tm)
    tok = lambda n, dt=BF16: (pl.BlockSpec((1, tm, n), lambda b, j: (b, j, 0)),
                              jax.ShapeDtypeStruct((B, L, n), dt))
    outs = [tok(p['wq'].shape[1]), tok(p['wk'].shape[1]), tok(p['wv'].shape[1]),
            tok(p['wg'].shape[1]), tok(2 * GLA_RANK, F32), tok(p['nq'].shape[1]),
            tok(p['nkv'].shape[1]), tok(MLA_ROPE), tok(D), tok(D)]
    weights = [p['wq'], p['wk'], p['wv'], p['wg'], p['wlat'], p['wmisc'], p['wga'], p['wgb']]
    return pl.pallas_call(
        _in_proj_kernel,
        grid=grid,
        in_specs=[pl.BlockSpec((1, tm, D), lambda b, j: (b, j, 0)),
                  _resident(p['g_pre'].shape),
                  pl.BlockSpec((tm, MLA_ROPE), lambda b, j: (j, 0)),
                  pl.BlockSpec((tm, MLA_ROPE), lambda b, j: (j, 0)),
                  _resident(p['nq'].shape), _resident(p['nkv'].shape)]
                 + [_resident(w.shape) for w in weights],
        out_specs=[o[0] for o in outs],
        out_shape=[o[1] for o in outs],
        compiler_params=_params(),
        name="in_proj",
    )(x, p['g_pre'], p['rope_cc'], p['rope_ss'], p['nq'], p['nkv'], *weights)


def _gla_kernel(q_ref, k_ref, v_ref, g_ref, dec_ref, waf_ref, baf_ref, wab_ref, bab_ref,
                gn_ref, o_ref, mf_ref, mb_ref, sf_ref, sb_ref, oacc_ref):
    L = q_ref.shape[1]
    C = GLA_CHUNK
    nc = L // C
    mid = C // 2 - 1
    scale = GLA_DK ** -0.5

    row = lax.broadcasted_iota(jnp.int32, (C, C), 0)
    col = lax.broadcasted_iota(jnp.int32, (C, C), 1)
    one = lambda c: jnp.where(c, 1.0, 0.0)
    mf_ref[...] = jnp.concatenate(
        [one(col <= row), one(col <= row) - one(col <= mid), one(col > row)], axis=0).astype(BF16)
    mb_ref[...] = jnp.concatenate(
        [one(col >= row), one(col >= row) - one(col > mid), one(col < row)], axis=0).astype(BF16)
    sf_ref[...] = jnp.zeros_like(sf_ref)
    sb_ref[...] = jnp.zeros_like(sb_ref)
    oacc_ref[...] = jnp.zeros_like(oacc_ref)

    def direction(c, m_ref, wa_ref, ba_ref, s_ref, mask, total_row):
        rows = pl.ds(pl.multiple_of(c * C, C), C)
        z = _dot(dec_ref[0, rows, :].astype(BF16), wa_ref[...]) + ba_ref[...]
        la = (jnp.minimum(z, 0.0) - jnp.log1p(jnp.exp(-jnp.abs(z)))) * (1.0 / GLA_TEMP)
        hi = la.astype(BF16)
        lo = (la - hi.astype(F32)).astype(BF16)
        cs = _dot(m_ref[...], hi) + _dot(m_ref[...], lo)
        b_incl, b_rel, b_rem = cs[:C], cs[C:2 * C], cs[2 * C:]
        q = q_ref[0, rows, :].astype(F32) * scale
        k = k_ref[0, rows, :].astype(F32)
        v = v_ref[0, rows, :]
        q_inter = (q * jnp.exp(b_incl)).astype(BF16)
        q_intra = (q * jnp.exp(b_rel)).astype(BF16)
        k_intra = (k * jnp.exp(-b_rel)).astype(BF16)
        k_end = k * jnp.exp(b_rem)
        a = jnp.where(mask, _dot_nt(q_intra, k_intra), 0.0).astype(BF16)
        s = s_ref[...]
        oacc_ref[rows, :] += _dot(a, v) + _dot(q_inter, s.astype(BF16))
        dec = jnp.exp(b_incl[total_row:total_row + 1, :])
        dec_t = jnp.broadcast_to(dec, (C, GLA_DK)).T
        dec_t = jnp.concatenate([dec_t] * (GLA_DV // C), axis=1)
        s_ref[...] = s * dec_t + _dot(k_end.T.astype(BF16), v)

    def step(i, carry):
        direction(i, mf_ref, waf_ref, baf_ref, sf_ref, row >= col, C - 1)
        direction(nc - 1 - i, mb_ref, wab_ref, bab_ref, sb_ref, row <= col, 0)
        return carry

    lax.fori_loop(0, nc, step, 0)

    nr = min(NORM_ROWS, L)
    for r in range(L // nr):
        rows = pl.ds(r * nr, nr)
        y = _rms(oacc_ref[rows, :], gn_ref[...])
        g = g_ref[0, rows, :].astype(F32)
        o_ref[0, rows, :] = (y * (g * jax.nn.sigmoid(g))).astype(BF16)


def _gla(q, k, v, g, dec, p):
    B, L, _ = q.shape
    C = GLA_CHUNK
    head = lambda n: pl.BlockSpec((1, L, n), lambda b, h: (b, 0, h))
    wspec = lambda a: pl.BlockSpec((a.shape[0], GLA_DK), lambda b, h: (0, h))
    return pl.pallas_call(
        _gla_kernel,
        grid=(B, GLA_HEADS),
        in_specs=[head(GLA_DK), head(GLA_DK), head(GLA_DV), head(GLA_DV),
                  pl.BlockSpec((1, L, 2 * GLA_RANK), lambda b, h: (b, 0, 0)),
                  wspec(p['wa_f']), wspec(p['ba_f']), wspec(p['wa_b']), wspec(p['ba_b']),
                  pl.BlockSpec((1, GLA_DV), lambda b, h: (0, 0))],
        out_specs=head(GLA_DV),
        out_shape=jax.ShapeDtypeStruct((B, L, GLA_HEADS * GLA_DV), BF16),
        scratch_shapes=[pltpu.VMEM((3 * C, C), BF16), pltpu.VMEM((3 * C, C), BF16),
                        pltpu.VMEM((GLA_DK, GLA_DV), F32), pltpu.VMEM((GLA_DK, GLA_DV), F32),
                        pltpu.VMEM((L, GLA_DV), F32)],
        compiler_params=_params(),
        name="gla",
    )(q, k, v, g, dec, p['wa_f'], p['ba_f'], p['wa_b'], p['ba_b'], p['gla_norm'])


def _mla_kernel(cq_ref, ckv_ref, kpe_ref, cc_ref, ss_ref, wqn_ref, wqp_ref, wqps_ref,
                wuk_ref, wuv_ref, o_ref, qs_ref, ks_ref, vs_ref):
    L = cq_ref.shape[1]
    scale = (MLA_NOPE + MLA_ROPE) ** -0.5
    cq = cq_ref[0]
    ckv = ckv_ref[0]
    qs_ref[:, :MLA_NOPE] = (_dot(cq, wqn_ref[...]) * scale).astype(BF16)
    qp = _dot(cq, wqp_ref[0]) * cc_ref[...] + _dot(cq, wqps_ref[0]) * ss_ref[...]
    qs_ref[:, MLA_NOPE:] = (qp * scale).astype(BF16)
    ks_ref[:, :MLA_NOPE] = _dot(ckv, wuk_ref[...]).astype(BF16)
    ks_ref[:, MLA_NOPE:] = kpe_ref[0]
    vs_ref[...] = _dot(ckv, wuv_ref[...]).astype(BF16)

    tq = min(Q_TOKENS, L)

    def step(i, carry):
        rows = pl.ds(pl.multiple_of(i * tq, tq), tq)
        s = _dot_nt(qs_ref[rows, :], ks_ref[...])
        m = jnp.max(s, axis=-1, keepdims=True)
        e = jnp.exp(s - m)
        l = jnp.sum(e, axis=-1, keepdims=True)
        o = _dot(e.astype(BF16), vs_ref[...]) * (1.0 / l)
        o_ref[0, rows, :] = o.astype(BF16)
        return carry

    lax.fori_loop(0, L // tq, step, 0)


def _mla(cq, ckv, kpe, p):
    B, L, R = cq.shape
    dqk = MLA_NOPE + MLA_ROPE
    seq = lambda n: pl.BlockSpec((1, L, n), lambda b, h: (b, 0, 0))
    return pl.pallas_call(
        _mla_kernel,
        grid=(B, MLA_HEADS),
        in_specs=[seq(R), seq(ckv.shape[2]), seq(MLA_ROPE),
                  _resident(p['rope_cc'].shape), _resident(p['rope_ss'].shape),
                  pl.BlockSpec((R, MLA_NOPE), lambda b, h: (0, h)),
                  pl.BlockSpec((1, R, MLA_ROPE), lambda b, h: (h, 0, 0)),
                  pl.BlockSpec((1, R, MLA_ROPE), lambda b, h: (h, 0, 0)),
                  pl.BlockSpec((ckv.shape[2], MLA_NOPE), lambda b, h: (0, h)),
                  pl.BlockSpec((ckv.shape[2], MLA_V), lambda b, h: (0, h))],
        out_specs=pl.BlockSpec((1, L, MLA_V), lambda b, h: (b, 0, h)),
        out_shape=jax.ShapeDtypeStruct((B, L, MLA_HEADS * MLA_V), BF16),
        scratch_shapes=[pltpu.VMEM((L, dqk), BF16), pltpu.VMEM((L, dqk), BF16),
                        pltpu.VMEM((L, MLA_V), BF16)],
        compiler_params=_params(),
        name="mla",
    )(cq, ckv, kpe, p['rope_cc'], p['rope_ss'], p['wq_nope'], p['wq_pe'], p['wq_pe_sw'],
      p['w_uk'], p['w_uv'])


def _out_kernel(x_ref, oa_ref, ob_ref, ga_ref, gb_ref, woa_ref, wob_ref, wout_ref,
                wgate_ref, wup_ref, wdown_ref, npost_ref, nfpre_ref, nfpost_ref, y_ref):
    ya = _dot(oa_ref[0], woa_ref[...])
    yb = _dot(ob_ref[0], wob_ref[...])
    merged = ga_ref[0].astype(F32) * ya + gb_ref[0].astype(F32) * yb
    x1 = x_ref[0] + _rms(_dot(merged.astype(BF16), wout_ref[...]), npost_ref[...])
    h = _rms(x1, nfpre_ref[...]).astype(BF16)
    gate = _dot(h, wgate_ref[...])
    act = (gate * jax.nn.sigmoid(gate) * _dot(h, wup_ref[...])).astype(BF16)
    y_ref[0] = x1 + _rms(_dot(act, wdown_ref[...]), nfpost_ref[...])


def _out(x, oa, ob, ga, gb, p):
    B, L, D = x.shape
    tm = min(OUT_TOKENS, L)
    tok = pl.BlockSpec((1, tm, D), lambda b, j: (b, j, 0))
    consts = [p['w_o_gla'], p['w_o_mla'], p['w_out'], p['w_gate'], p['w_up'], p['w_down'],
              p['g_post'], p['g_ffn_pre'], p['g_ffn_post']]
    return pl.pallas_call(
        _out_kernel,
        grid=(B, L // tm),
        in_specs=[tok] * 5 + [_resident(c.shape) for c in consts],
        out_specs=tok,
        out_shape=jax.ShapeDtypeStruct((B, L, D), F32),
        compiler_params=_params(),
        name="merge_ffn",
    )(x, oa, ob, ga, gb, *consts)


def _prepare(L, norm_mix_pre, w_in, gla_wa_fwd, gla_ba_fwd, gla_wa_bwd, gla_ba_bwd, gla_norm,
             w_o_gla, mla_norm_q, w_uq, mla_norm_kv, w_uk, w_uv, w_o_mla, w_out,
             norm_mix_post, norm_ffn_pre, w_gate, w_up, w_down, norm_ffn_post):
    D = w_in.shape[0]
    qk = GLA_HEADS * GLA_DK
    vw = GLA_HEADS * GLA_DV
    rq, rkv = mla_norm_q.shape[0], mla_norm_kv.shape[0]
    widths = (qk, qk, vw, vw, GLA_RANK, GLA_RANK, rq, rkv, MLA_ROPE, D, D)
    offs = np.concatenate([[0], np.cumsum(widths)]).tolist()
    piece = lambda i, j=None: w_in[:, offs[i]:offs[(i if j is None else j) + 1]]
    swap = np.concatenate([np.arange(MLA_ROPE // 2, MLA_ROPE), np.arange(MLA_ROPE // 2)])
    w_kr = piece(8)
    row = lambda a: a.reshape(1, -1).astype(F32)

    inv = ROPE_BASE ** (-jnp.arange(0, MLA_ROPE, 2, dtype=F32) / MLA_ROPE)
    ang = jnp.arange(L, dtype=F32)[:, None] * inv[None, :]
    cos, sin = jnp.cos(ang), jnp.sin(ang)

    uq = w_uq.reshape(rq, MLA_HEADS, MLA_NOPE + MLA_ROPE)
    wq_pe = jnp.transpose(uq[:, :, MLA_NOPE:], (1, 0, 2))
    zeros = jnp.zeros((GLA_RANK, qk), F32)
    return dict(
        g_pre=row(norm_mix_pre),
        wq=piece(0).astype(BF16), wk=piece(1).astype(BF16), wv=piece(2).astype(BF16),
        wg=piece(3).astype(BF16), wlat=piece(6, 7).astype(BF16),
        wmisc=jnp.concatenate([w_kr, w_kr[:, swap], piece(4, 5)], axis=1).astype(BF16),
        wga=piece(9).astype(BF16), wgb=piece(10).astype(BF16),
        nq=row(mla_norm_q), nkv=row(mla_norm_kv),
        rope_cc=jnp.concatenate([cos, cos], axis=1), rope_ss=jnp.concatenate([-sin, sin], axis=1),
        wa_f=jnp.concatenate([gla_wa_fwd, zeros], axis=0).astype(BF16),
        wa_b=jnp.concatenate([zeros, gla_wa_bwd], axis=0).astype(BF16),
        ba_f=row(gla_ba_fwd), ba_b=row(gla_ba_bwd), gla_norm=row(gla_norm),
        wq_nope=uq[:, :, :MLA_NOPE].reshape(rq, MLA_HEADS * MLA_NOPE).astype(BF16),
        wq_pe=wq_pe.astype(BF16), wq_pe_sw=wq_pe[:, :, swap].astype(BF16),
        w_uk=w_uk.astype(BF16), w_uv=w_uv.astype(BF16),
        w_o_gla=w_o_gla.astype(BF16), w_o_mla=w_o_mla.astype(BF16), w_out=w_out.astype(BF16),
        w_gate=w_gate.astype(BF16), w_up=w_up.astype(BF16), w_down=w_down.astype(BF16),
        g_post=row(norm_mix_post), g_ffn_pre=row(norm_ffn_pre), g_ffn_post=row(norm_ffn_post),
    )


def _layer(x, p):
    q, k, v, g, dec, cq, ckv, kpe, ga, gb = _in_proj(x, p)
    oa = _gla(q, k, v, g, dec, p)
    ob = _mla(cq, ckv, kpe, p)
    return _out(x, oa, ob, ga, gb, p)


def kernel(x_prompt, x_sample, norm_mix_pre, w_in, gla_wa_fwd, gla_ba_fwd, gla_wa_bwd, gla_ba_bwd, gla_norm, w_o_gla, mla_norm_q, w_uq, mla_norm_kv, w_uk, w_uv, w_o_mla, w_out, norm_mix_post, norm_ffn_pre, w_gate, w_up, w_down, norm_ffn_post):
    assert x_prompt.shape[1] == x_sample.shape[1]
    stacked = (norm_mix_pre, w_in, gla_wa_fwd, gla_ba_fwd, gla_wa_bwd, gla_ba_bwd, gla_norm,
               w_o_gla, mla_norm_q, w_uq, mla_norm_kv, w_uk, w_uv, w_o_mla, w_out,
               norm_mix_post, norm_ffn_pre, w_gate, w_up, w_down, norm_ffn_post)
    y_prompt, y_sample = x_prompt, x_sample
    for layer in range(norm_mix_pre.shape[0]):
        p = _prepare(x_prompt.shape[1], *[a[layer] for a in stacked])
        y_prompt, y_sample = _layer(y_prompt, p), _layer(y_sample, p)
    return (y_prompt, y_sample)
```

```python
import jax
import jax.numpy as jnp
import numpy as np
from jax import lax
from jax.experimental import pallas as pl
from jax.experimental.pallas import tpu as pltpu

F32 = jnp.float32
BF16 = jnp.bfloat16

EPS = 1e-6
GLA_HEADS = 4
GLA_DK = 128
GLA_DV = 256
GLA_RANK = 16
GLA_TEMP = 16.0
GLA_CHUNK = 128
MLA_HEADS = 8
MLA_NOPE = 128
MLA_ROPE = 64
MLA_V = 128
ROPE_BASE = 10000.0

IN_TOKENS = 512
OUT_TOKENS = 256
Q_TOKENS = 256
GLA_UNROLL = 4
NORM_ROWS = 256
VMEM_LIMIT_BYTES = 56 * 1024 * 1024


def _rms(xf, g):
    ms = jnp.mean(xf * xf, axis=-1, keepdims=True)
    return xf * lax.rsqrt(ms + EPS) * g


def _dot(a, b):
    return jnp.dot(a, b, preferred_element_type=F32)


def _dot_nt(a, b):
    return lax.dot_general(a, b, (((1,), (1,)), ((), ())), preferred_element_type=F32)


def _rope(x2, table):
    y = x2 * table
    return (y + pltpu.roll(y, MLA_ROPE, axis=1))[:, :MLA_ROPE]


def _params():
    return pltpu.CompilerParams(
        dimension_semantics=("arbitrary", "arbitrary"),
        vmem_limit_bytes=VMEM_LIMIT_BYTES,
    )


def _resident(shape):
    nd = len(shape)
    return pl.BlockSpec(shape, lambda *_: (0,) * nd, pipeline_mode=pl.Buffered(1))


def _in_proj_kernel(x_ref, g_ref, rope_ref, nq_ref, nkv_ref,
                    wq_ref, wk_ref, wv_ref, wg_ref, wlat_ref, wmisc_ref, wga_ref, wgb_ref,
                    q_out, k_out, v_out, g_out, dec_out, cq_out, ckv_out, kpe_out,
                    ga_out, gb_out):
    h = _rms(x_ref[0], g_ref[...]).astype(BF16)

    q_out[0] = _dot(h, wq_ref[...]).astype(BF16)
    k_out[0] = _dot(h, wk_ref[...]).astype(BF16)
    v_out[0] = _dot(h, wv_ref[...]).astype(BF16)
    g_out[0] = _dot(h, wg_ref[...]).astype(BF16)

    lat = _dot(h, wlat_ref[...])
    r = nq_ref.shape[1]
    cq_out[0] = _rms(lat[:, :r], nq_ref[...]).astype(BF16)
    ckv_out[0] = _rms(lat[:, r:], nkv_ref[...]).astype(BF16)

    misc = _dot(h, wmisc_ref[...])
    kpe_out[0] = _rope(misc[:, :2 * MLA_ROPE], rope_ref[...]).astype(BF16)
    dec_out[0] = misc[:, 2 * MLA_ROPE:2 * MLA_ROPE + 2 * GLA_RANK]

    ga_out[0] = jax.nn.sigmoid(_dot(h, wga_ref[...])).astype(BF16)
    gb_out[0] = jax.nn.sigmoid(_dot(h, wgb_ref[...])).astype(BF16)


def _in_proj(x, p):
    B, L, D = x.shape
    tm = min(IN_TOKENS, L)
    grid = (B, L // tm)
    tok = lambda n, dt=BF16: (pl.BlockSpec((1, tm, n), lambda b, j: (b, j, 0)),
                              jax.ShapeDtypeStruct((B, L, n), dt))
    outs = [tok(p['wq'].shape[1]), tok(p['wk'].shape[1]), tok(p['wv'].shape[1]),
            tok(p['wg'].shape[1]), tok(2 * GLA_RANK, F32), tok(p['nq'].shape[1]),
            tok(p['nkv'].shape[1]), tok(MLA_ROPE), tok(D), tok(D)]
    weights = [p['wq'], p['wk'], p['wv'], p['wg'], p['wlat'], p['wmisc'], p['wga'], p['wgb']]
    return pl.pallas_call(
        _in_proj_kernel,
        grid=grid,
        in_specs=[pl.BlockSpec((1, tm, D), lambda b, j: (b, j, 0)),
                  _resident(p['g_pre'].shape),
                  pl.BlockSpec((tm, 2 * MLA_ROPE), lambda b, j: (j, 0)),
                  _resident(p['nq'].shape), _resident(p['nkv'].shape)]
                 + [_resident(w.shape) for w in weights],
        out_specs=[o[0] for o in outs],
        out_shape=[o[1] for o in outs],
        compiler_params=_params(),
        name="in_proj",
    )(x, p['g_pre'], p['rope'], p['nq'], p['nkv'], *weights)


def _gla_kernel(q_ref, k_ref, v_ref, g_ref, dec_ref, waf_ref, baf_ref, wab_ref, bab_ref,
                gn_ref, o_ref, mf_ref, mb_ref, sf_ref, sb_ref, hlf_ref, hlb_ref, bf_ref, bb_ref,
                qif_ref, qib_ref, uf_ref, ub_ref, dtf_ref, dtb_ref, a_ref, oacc_ref):
    L = q_ref.shape[1]
    C = GLA_CHUNK
    nc = L // C
    nr = min(NORM_ROWS, L)
    mid = C // 2 - 1
    scale = GLA_DK ** -0.5

    row = lax.broadcasted_iota(jnp.int32, (C, C), 0)
    col = lax.broadcasted_iota(jnp.int32, (C, C), 1)
    mf_ref[...] = jnp.where(col <= row, 1.0, 0.0).astype(BF16)
    mb_ref[...] = jnp.where(col >= row, 1.0, 0.0).astype(BF16)

    def decays(r, carry):
        rows = pl.ds(pl.multiple_of(r * nr, nr), nr)
        dl = dec_ref[0, rows, :].astype(BF16)
        for wa_ref, ba_ref, hl_ref in ((waf_ref, baf_ref, hlf_ref), (wab_ref, bab_ref, hlb_ref)):
            z = _dot(dl, wa_ref[...]) + ba_ref[...]
            la = (jnp.minimum(z, 0.0) - jnp.log(1.0 + jnp.exp(-jnp.abs(z)))) * (1.0 / GLA_TEMP)
            hi = la.astype(BF16)
            hl_ref[rows, :GLA_DK] = hi
            hl_ref[rows, GLA_DK:] = (la - hi.astype(F32)).astype(BF16)
        return carry

    lax.fori_loop(0, L // nr, decays, 0, unroll=2)

    def cumsum(c, carry):
        rows = pl.ds(pl.multiple_of(c * C, C), C)
        for m_ref, hl_ref, b_ref in ((mf_ref, hlf_ref, bf_ref), (mb_ref, hlb_ref, bb_ref)):
            cs = _dot(m_ref[...], hl_ref[rows, :])
            b_ref[rows, :] = cs[:, :GLA_DK] + cs[:, GLA_DK:]
        return carry

    lax.fori_loop(0, nc, cumsum, 0, unroll=GLA_UNROLL)

    dirs = ((bf_ref, qif_ref, uf_ref, dtf_ref, C - 1, mid, True),
            (bb_ref, qib_ref, ub_ref, dtb_ref, 0, mid + 1, False))

    def intra(c, carry):
        rows = pl.ds(pl.multiple_of(c * C, C), C)
        q = q_ref[0, rows, :].astype(F32) * scale
        k = k_ref[0, rows, :].astype(F32)
        v = v_ref[0, rows, :]
        a_sum = None
        for b_ref, qi_ref, u_ref, dt_ref, tot_row, ref_row, fwd in dirs:
            b_incl = b_ref[rows, :]
            b_tot = b_incl[tot_row:tot_row + 1, :]
            b_rel = b_incl - b_incl[ref_row:ref_row + 1, :]
            qi_ref[rows, :] = (q * jnp.exp(b_incl)).astype(BF16)
            q_intra = (q * jnp.exp(b_rel)).astype(BF16)
            k_intra = (k * jnp.exp(-b_rel)).astype(BF16)
            k_end = k * jnp.exp(b_tot - b_incl)
            mask = (row >= col) if fwd else (row <= col)
            a = jnp.where(mask, _dot_nt(q_intra, k_intra), 0.0)
            a_sum = a if a_sum is None else a_sum + a
            u_ref[rows, :] = _dot(k_end.T.astype(BF16), v)
            dt_ref[rows, :] = jnp.broadcast_to(jnp.exp(b_tot), (C, GLA_DK)).T
        a_ref[rows, :] = a_sum.astype(BF16)
        return carry

    lax.fori_loop(0, nc, intra, 0, unroll=2)

    def intra_out(c, carry):
        rows = pl.ds(pl.multiple_of(c * C, C), C)
        oacc_ref[rows, :] = _dot(a_ref[rows, :], v_ref[0, rows, :])
        return carry

    lax.fori_loop(0, nc, intra_out, 0, unroll=GLA_UNROLL)

    sf_ref[...] = jnp.zeros_like(sf_ref)
    sb_ref[...] = jnp.zeros_like(sb_ref)

    def scan(i, carry):
        for s_ref, qi_ref, u_ref, dt_ref, c in ((sf_ref, qif_ref, uf_ref, dtf_ref, i),
                                                (sb_ref, qib_ref, ub_ref, dtb_ref, nc - 1 - i)):
            rows = pl.ds(pl.multiple_of(c * C, C), C)
            s = s_ref[...]
            oacc_ref[rows, :] += _dot(qi_ref[rows, :], s.astype(BF16))
            dt = dt_ref[rows, :]
            s_ref[...] = s * jnp.concatenate([dt] * (GLA_DV // C), axis=1) + u_ref[rows, :]
        return carry

    lax.fori_loop(0, nc, scan, 0, unroll=GLA_UNROLL)

    for r in range(L // nr):
        rows = pl.ds(r * nr, nr)
        y = _rms(oacc_ref[rows, :], gn_ref[...])
        g = g_ref[0, rows, :].astype(F32)
        o_ref[0, rows, :] = (y * (g * jax.nn.sigmoid(g))).astype(BF16)


def _gla(q, k, v, g, dec, p):
    B, L, _ = q.shape
    C = GLA_CHUNK
    head = lambda n: pl.BlockSpec((1, L, n), lambda b, h: (b, 0, h))
    wspec = lambda a: pl.BlockSpec((a.shape[0], GLA_DK), lambda b, h: (0, h))
    return pl.pallas_call(
        _gla_kernel,
        grid=(B, GLA_HEADS),
        in_specs=[head(GLA_DK), head(GLA_DK), head(GLA_DV), head(GLA_DV),
                  pl.BlockSpec((1, L, 2 * GLA_RANK), lambda b, h: (b, 0, 0)),
                  wspec(p['wa_f']), wspec(p['ba_f']), wspec(p['wa_b']), wspec(p['ba_b']),
                  pl.BlockSpec((1, GLA_DV), lambda b, h: (0, 0))],
        out_specs=head(GLA_DV),
        out_shape=jax.ShapeDtypeStruct((B, L, GLA_HEADS * GLA_DV), BF16),
        scratch_shapes=[pltpu.VMEM((C, C), BF16), pltpu.VMEM((C, C), BF16),
                        pltpu.VMEM((GLA_DK, GLA_DV), F32), pltpu.VMEM((GLA_DK, GLA_DV), F32),
                        pltpu.VMEM((L, 2 * GLA_DK), BF16), pltpu.VMEM((L, 2 * GLA_DK), BF16),
                        pltpu.VMEM((L, GLA_DK), F32), pltpu.VMEM((L, GLA_DK), F32),
                        pltpu.VMEM((L, GLA_DK), BF16), pltpu.VMEM((L, GLA_DK), BF16),
                        pltpu.VMEM((L, GLA_DV), F32), pltpu.VMEM((L, GLA_DV), F32),
                        pltpu.VMEM((L, GLA_DK), F32), pltpu.VMEM((L, GLA_DK), F32),
                        pltpu.VMEM((L, C), BF16), pltpu.VMEM((L, GLA_DV), F32)],
        compiler_params=_params(),
        name="gla",
    )(q, k, v, g, dec, p['wa_f'], p['ba_f'], p['wa_b'], p['ba_b'], p['gla_norm'])


def _mla_kernel(cq_ref, ckv_ref, kpe_ref, rope_ref, wq_ref, wkv_ref, o_ref,
                qs_ref, ks_ref, vs_ref):
    L = cq_ref.shape[1]
    scale = (MLA_NOPE + MLA_ROPE) ** -0.5
    qall = _dot(cq_ref[0], wq_ref[0])
    qs_ref[:, :MLA_NOPE] = (qall[:, :MLA_NOPE] * scale).astype(BF16)
    qs_ref[:, MLA_NOPE:] = (_rope(qall[:, MLA_NOPE:], rope_ref[...]) * scale).astype(BF16)
    kv = _dot(ckv_ref[0], wkv_ref[0])
    ks_ref[:, :MLA_NOPE] = kv[:, :MLA_NOPE].astype(BF16)
    ks_ref[:, MLA_NOPE:] = kpe_ref[0]
    vs_ref[...] = kv[:, MLA_NOPE:].astype(BF16)

    tq = min(Q_TOKENS, L)
    nq = L // tq
    scores = lambda i: _dot_nt(qs_ref[i * tq:(i + 1) * tq, :], ks_ref[...])
    s = scores(0)
    for i in range(nq):
        s_next = scores(i + 1) if i + 1 < nq else None
        m = jnp.max(s, axis=-1, keepdims=True)
        e = jnp.exp(s - m)
        l = jnp.sum(e, axis=-1, keepdims=True)
        o = _dot(e.astype(BF16), vs_ref[...]) * (1.0 / l)
        o_ref[0, i * tq:(i + 1) * tq, :] = o.astype(BF16)
        s = s_next


def _mla(cq, ckv, kpe, p):
    B, L, R = cq.shape
    dqk = MLA_NOPE + MLA_ROPE
    seq = lambda n: pl.BlockSpec((1, L, n), lambda b, h: (b, 0, 0))
    per_head = lambda a: pl.BlockSpec((1,) + a.shape[1:], lambda b, h: (h, 0, 0))
    return pl.pallas_call(
        _mla_kernel,
        grid=(B, MLA_HEADS),
        in_specs=[seq(R), seq(ckv.shape[2]), seq(MLA_ROPE), _resident(p['rope'].shape),
                  per_head(p['wq_mla']), per_head(p['wkv_mla'])],
        out_specs=pl.BlockSpec((1, L, MLA_V), lambda b, h: (b, 0, h)),
        out_shape=jax.ShapeDtypeStruct((B, L, MLA_HEADS * MLA_V), BF16),
        scratch_shapes=[pltpu.VMEM((L, dqk), BF16), pltpu.VMEM((L, dqk), BF16),
                        pltpu.VMEM((L, MLA_V), BF16)],
        compiler_params=_params(),
        name="mla",
    )(cq, ckv, kpe, p['rope'], p['wq_mla'], p['wkv_mla'])


def _out_kernel(x_ref, oa_ref, ob_ref, ga_ref, gb_ref, woa_ref, wob_ref, wout_ref,
                wgate_ref, wup_ref, wdown_ref, npost_ref, nfpre_ref, nfpost_ref, y_ref):
    ya = _dot(oa_ref[0], woa_ref[...])
    yb = _dot(ob_ref[0], wob_ref[...])
    merged = ga_ref[0].astype(F32) * ya + gb_ref[0].astype(F32) * yb
    x1 = x_ref[0] + _rms(_dot(merged.astype(BF16), wout_ref[...]), npost_ref[...])
    h = _rms(x1, nfpre_ref[...]).astype(BF16)
    gate = _dot(h, wgate_ref[...])
    act = (gate * jax.nn.sigmoid(gate) * _dot(h, wup_ref[...])).astype(BF16)
    y_ref[0] = x1 + _rms(_dot(act, wdown_ref[...]), nfpost_ref[...])


def _out(x, oa, ob, ga, gb, p):
    B, L, D = x.shape
    tm = min(OUT_TOKENS, L)
    tok = pl.BlockSpec((1, tm, D), lambda b, j: (b, j, 0))
    consts = [p['w_o_gla'], p['w_o_mla'], p['w_out'], p['w_gate'], p['w_up'], p['w_down'],
              p['g_post'], p['g_ffn_pre'], p['g_ffn_post']]
    return pl.pallas_call(
        _out_kernel,
        grid=(B, L // tm),
        in_specs=[tok] * 5 + [_resident(c.shape) for c in consts],
        out_specs=tok,
        out_shape=jax.ShapeDtypeStruct((B, L, D), F32),
        compiler_params=_params(),
        name="merge_ffn",
    )(x, oa, ob, ga, gb, *consts)


def _prepare(L, norm_mix_pre, w_in, gla_wa_fwd, gla_ba_fwd, gla_wa_bwd, gla_ba_bwd, gla_norm,
             w_o_gla, mla_norm_q, w_uq, mla_norm_kv, w_uk, w_uv, w_o_mla, w_out,
             norm_mix_post, norm_ffn_pre, w_gate, w_up, w_down, norm_ffn_post):
    D = w_in.shape[0]
    qk = GLA_HEADS * GLA_DK
    vw = GLA_HEADS * GLA_DV
    rq, rkv = mla_norm_q.shape[0], mla_norm_kv.shape[0]
    widths = (qk, qk, vw, vw, GLA_RANK, GLA_RANK, rq, rkv, MLA_ROPE, D, D)
    offs = np.concatenate([[0], np.cumsum(widths)]).tolist()
    piece = lambda i, j=None: w_in[:, offs[i]:offs[(i if j is None else j) + 1]]
    swap = np.concatenate([np.arange(MLA_ROPE // 2, MLA_ROPE), np.arange(MLA_ROPE // 2)])
    w_kr = piece(8)
    row = lambda a: a.reshape(1, -1).astype(F32)

    inv = ROPE_BASE ** (-jnp.arange(0, MLA_ROPE, 2, dtype=F32) / MLA_ROPE)
    ang = jnp.arange(L, dtype=F32)[:, None] * inv[None, :]
    cos, sin = jnp.cos(ang), jnp.sin(ang)

    uq = w_uq.reshape(rq, MLA_HEADS, MLA_NOPE + MLA_ROPE)
    wq_pe = uq[:, :, MLA_NOPE:]
    wq_mla = jnp.concatenate([uq[:, :, :MLA_NOPE], wq_pe, wq_pe[:, :, swap]], axis=2)
    wkv_mla = jnp.concatenate([w_uk.reshape(rkv, MLA_HEADS, MLA_NOPE),
                               w_uv.reshape(rkv, MLA_HEADS, MLA_V)], axis=2)
    zeros = jnp.zeros((GLA_RANK, qk), F32)
    return dict(
        g_pre=row(norm_mix_pre),
        wq=piece(0).astype(BF16), wk=piece(1).astype(BF16), wv=piece(2).astype(BF16),
        wg=piece(3).astype(BF16), wlat=piece(6, 7).astype(BF16),
        wmisc=jnp.concatenate([w_kr, w_kr[:, swap], piece(4, 5)], axis=1).astype(BF16),
        wga=piece(9).astype(BF16), wgb=piece(10).astype(BF16),
        nq=row(mla_norm_q), nkv=row(mla_norm_kv),
        rope=jnp.concatenate([cos, cos, -sin, sin], axis=1),
        wa_f=jnp.concatenate([gla_wa_fwd, zeros], axis=0).astype(BF16),
        wa_b=jnp.concatenate([zeros, gla_wa_bwd], axis=0).astype(BF16),
        ba_f=row(gla_ba_fwd), ba_b=row(gla_ba_bwd), gla_norm=row(gla_norm),
        wq_mla=jnp.transpose(wq_mla, (1, 0, 2)).astype(BF16),
        wkv_mla=jnp.transpose(wkv_mla, (1, 0, 2)).astype(BF16),
        w_o_gla=w_o_gla.astype(BF16), w_o_mla=w_o_mla.astype(BF16), w_out=w_out.astype(BF16),
        w_gate=w_gate.astype(BF16), w_up=w_up.astype(BF16), w_down=w_down.astype(BF16),
        g_post=row(norm_mix_post), g_ffn_pre=row(norm_ffn_pre), g_ffn_post=row(norm_ffn_post),
    )


def _layer(x, p):
    q, k, v, g, dec, cq, ckv, kpe, ga, gb = _in_proj(x, p)
    oa = _gla(q, k, v, g, dec, p)
    ob = _mla(cq, ckv, kpe, p)
    return _out(x, oa, ob, ga, gb, p)


def kernel(x_prompt, x_sample, norm_mix_pre, w_in, gla_wa_fwd, gla_ba_fwd, gla_wa_bwd, gla_ba_bwd, gla_norm, w_o_gla, mla_norm_q, w_uq, mla_norm_kv, w_uk, w_uv, w_o_mla, w_out, norm_mix_post, norm_ffn_pre, w_gate, w_up, w_down, norm_ffn_post):
    assert x_prompt.shape[1] == x_sample.shape[1]
    stacked = (norm_mix_pre, w_in, gla_wa_fwd, gla_ba_fwd, gla_wa_bwd, gla_ba_bwd, gla_norm,
               w_o_gla, mla_norm_q, w_uq, mla_norm_kv, w_uk, w_uv, w_o_mla, w_out,
               norm_mix_post, norm_ffn_pre, w_gate, w_up, w_down, norm_ffn_post)
    y_prompt, y_sample = x_prompt, x_sample
    for layer in range(norm_mix_pre.shape[0]):
        p = _prepare(x_prompt.shape[1], *[a[layer] for a in stacked])
        y_prompt, y_sample = _layer(y_prompt, p), _layer(y_sample, p)
    return (y_prompt, y_sample)
```

```python
import jax
import jax.numpy as jnp
import numpy as np
from jax import lax
from jax.experimental import pallas as pl
from jax.experimental.pallas import tpu as pltpu

F32 = jnp.float32
BF16 = jnp.bfloat16

EPS = 1e-6
GLA_HEADS = 4
GLA_DK = 128
GLA_DV = 256
GLA_RANK = 16
GLA_TEMP = 16.0
GLA_CHUNK = 128
MLA_HEADS = 8
MLA_NOPE = 128
MLA_ROPE = 64
MLA_V = 128
ROPE_BASE = 10000.0

IN_TOKENS = 512
OUT_TOKENS = 512
OUT_PARTS = 2
Q_TOKENS = 256
KEY_CHUNK = 256
ONES_ROWS = 16
LOG2_E = 1.4426950408889634
GLA_UNROLL = 4
NORM_ROWS = 256
VMEM_LIMIT_BYTES = 56 * 1024 * 1024


def _rms(xf, g):
    ms = jnp.mean(xf * xf, axis=-1, keepdims=True)
    return xf * lax.rsqrt(ms + EPS) * g


def _dot(a, b):
    return jnp.dot(a, b, preferred_element_type=F32)


def _dot_nt(a, b):
    return lax.dot_general(a, b, (((1,), (1,)), ((), ())), preferred_element_type=F32)


def _rope(x2, table):
    y = x2 * table
    return (y + pltpu.roll(y, MLA_ROPE, axis=1))[:, :MLA_ROPE]


def _params():
    return pltpu.CompilerParams(
        dimension_semantics=("arbitrary", "arbitrary"),
        vmem_limit_bytes=VMEM_LIMIT_BYTES,
    )


def _resident(shape):
    nd = len(shape)
    return pl.BlockSpec(shape, lambda *_: (0,) * nd, pipeline_mode=pl.Buffered(1))


def _in_proj_kernel(x_ref, g_ref, rope_ref, nq_ref, nkv_ref,
                    wq_ref, wk_ref, wv_ref, wg_ref, wlat_ref, wmisc_ref, wga_ref, wgb_ref,
                    q_out, k_out, v_out, g_out, dec_out, cq_out, ckv_out, kpe_out,
                    ga_out, gb_out):
    h = _rms(x_ref[0], g_ref[...]).astype(BF16)

    ga_out[0] = jax.nn.sigmoid(_dot(h, wga_ref[...])).astype(BF16)
    gb_out[0] = jax.nn.sigmoid(_dot(h, wgb_ref[...])).astype(BF16)

    lat = _dot(h, wlat_ref[...])
    r = nq_ref.shape[1]
    cq_out[0] = _rms(lat[:, :r], nq_ref[...]).astype(BF16)
    ckv_out[0] = _rms(lat[:, r:], nkv_ref[...]).astype(BF16)

    misc = _dot(h, wmisc_ref[...])
    kpe_out[0] = _rope(misc[:, :2 * MLA_ROPE], rope_ref[...]).astype(BF16)
    dec_out[0] = misc[:, 2 * MLA_ROPE:2 * MLA_ROPE + 2 * GLA_RANK]

    v_out[0] = _dot(h, wv_ref[...]).astype(BF16)
    g_out[0] = _dot(h, wg_ref[...]).astype(BF16)
    q_out[0] = _dot(h, wq_ref[...]).astype(BF16)
    k_out[0] = _dot(h, wk_ref[...]).astype(BF16)


def _in_proj(x, p):
    B, L, D = x.shape
    tm = min(IN_TOKENS, L)
    grid = (B, L // tm)
    tok = lambda n, dt=BF16: (pl.BlockSpec((1, tm, n), lambda b, j: (b, j, 0)),
                              jax.ShapeDtypeStruct((B, L, n), dt))
    outs = [tok(p['wq'].shape[1]), tok(p['wk'].shape[1]), tok(p['wv'].shape[1]),
            tok(p['wg'].shape[1]), tok(2 * GLA_RANK, F32), tok(p['nq'].shape[1]),
            tok(p['nkv'].shape[1]), tok(MLA_ROPE), tok(D), tok(D)]
    weights = [p['wq'], p['wk'], p['wv'], p['wg'], p['wlat'], p['wmisc'], p['wga'], p['wgb']]
    return pl.pallas_call(
        _in_proj_kernel,
        grid=grid,
        in_specs=[pl.BlockSpec((1, tm, D), lambda b, j: (b, j, 0)),
                  _resident(p['g_pre'].shape),
                  pl.BlockSpec((tm, 2 * MLA_ROPE), lambda b, j: (j, 0)),
                  _resident(p['nq'].shape), _resident(p['nkv'].shape)]
                 + [_resident(w.shape) for w in weights],
        out_specs=[o[0] for o in outs],
        out_shape=[o[1] for o in outs],
        compiler_params=_params(),
        name="in_proj",
    )(x, p['g_pre'], p['rope'], p['nq'], p['nkv'], *weights)


def _gla_kernel(q_ref, k_ref, v_ref, g_ref, dec_ref, waf_ref, baf_ref, wab_ref, bab_ref,
                gn_ref, o_ref, mf_ref, mb_ref, sf_ref, sb_ref, hlf_ref, hlb_ref, bf_ref, bb_ref,
                qif_ref, qib_ref, uf_ref, ub_ref, dtf_ref, dtb_ref, a_ref, oacc_ref):
    L = q_ref.shape[1]
    C = GLA_CHUNK
    nc = L // C
    nr = min(NORM_ROWS, L)
    mid = C // 2 - 1
    scale = GLA_DK ** -0.5

    row = lax.broadcasted_iota(jnp.int32, (C, C), 0)
    col = lax.broadcasted_iota(jnp.int32, (C, C), 1)
    mf_ref[...] = jnp.where(col <= row, 1.0, 0.0).astype(BF16)
    mb_ref[...] = jnp.where(col >= row, 1.0, 0.0).astype(BF16)

    def decays(r, carry):
        rows = pl.ds(pl.multiple_of(r * nr, nr), nr)
        dl = dec_ref[0, rows, :].astype(BF16)
        for wa_ref, ba_ref, hl_ref in ((waf_ref, baf_ref, hlf_ref), (wab_ref, bab_ref, hlb_ref)):
            z = _dot(dl, wa_ref[...]) + ba_ref[...]
            la = (jnp.minimum(z, 0.0) - jnp.log(1.0 + jnp.exp(-jnp.abs(z)))) * (1.0 / GLA_TEMP)
            hi = la.astype(BF16)
            hl_ref[rows, :GLA_DK] = hi
            hl_ref[rows, GLA_DK:] = (la - hi.astype(F32)).astype(BF16)
        return carry

    lax.fori_loop(0, L // nr, decays, 0, unroll=2)

    def cumsum(c, carry):
        rows = pl.ds(pl.multiple_of(c * C, C), C)
        for m_ref, hl_ref, b_ref in ((mf_ref, hlf_ref, bf_ref), (mb_ref, hlb_ref, bb_ref)):
            cs = _dot(m_ref[...], hl_ref[rows, :])
            b_ref[rows, :] = cs[:, :GLA_DK] + cs[:, GLA_DK:]
        return carry

    lax.fori_loop(0, nc, cumsum, 0, unroll=GLA_UNROLL)

    dirs = ((bf_ref, qif_ref, uf_ref, dtf_ref, C - 1, mid, True),
            (bb_ref, qib_ref, ub_ref, dtb_ref, 0, mid + 1, False))

    def intra(c, carry):
        rows = pl.ds(pl.multiple_of(c * C, C), C)
        q = q_ref[0, rows, :].astype(F32) * scale
        k = k_ref[0, rows, :].astype(F32)
        v = v_ref[0, rows, :]
        a_sum = None
        for b_ref, qi_ref, u_ref, dt_ref, tot_row, ref_row, fwd in dirs:
            b_incl = b_ref[rows, :]
            b_tot = b_incl[tot_row:tot_row + 1, :]
            b_rel = b_incl - b_incl[ref_row:ref_row + 1, :]
            qi_ref[rows, :] = (q * jnp.exp(b_incl)).astype(BF16)
            q_intra = (q * jnp.exp(b_rel)).astype(BF16)
            k_intra = (k * jnp.exp(-b_rel)).astype(BF16)
            k_end = k * jnp.exp(b_tot - b_incl)
            mask = (row >= col) if fwd else (row <= col)
            a = jnp.where(mask, _dot_nt(q_intra, k_intra), 0.0)
            a_sum = a if a_sum is None else a_sum + a
            u_ref[rows, :] = _dot(k_end.T.astype(BF16), v)
            dt_ref[rows, :] = jnp.broadcast_to(jnp.exp(b_tot), (C, GLA_DK)).T
        a_ref[rows, :] = a_sum.astype(BF16)
        return carry

    lax.fori_loop(0, nc, intra, 0, unroll=2)

    def intra_out(c, carry):
        rows = pl.ds(pl.multiple_of(c * C, C), C)
        oacc_ref[rows, :] = _dot(a_ref[rows, :], v_ref[0, rows, :])
        return carry

    lax.fori_loop(0, nc, intra_out, 0, unroll=GLA_UNROLL)

    sf_ref[...] = jnp.zeros_like(sf_ref)
    sb_ref[...] = jnp.zeros_like(sb_ref)

    def scan(i, carry):
        for s_ref, qi_ref, u_ref, dt_ref, c in ((sf_ref, qif_ref, uf_ref, dtf_ref, i),
                                                (sb_ref, qib_ref, ub_ref, dtb_ref, nc - 1 - i)):
            rows = pl.ds(pl.multiple_of(c * C, C), C)
            s = s_ref[...]
            oacc_ref[rows, :] += _dot(qi_ref[rows, :], s.astype(BF16))
            dt = dt_ref[rows, :]
            s_ref[...] = s * jnp.concatenate([dt] * (GLA_DV // C), axis=1) + u_ref[rows, :]
        return carry

    lax.fori_loop(0, nc, scan, 0, unroll=GLA_UNROLL)

    for r in range(L // nr):
        rows = pl.ds(r * nr, nr)
        y = _rms(oacc_ref[rows, :], gn_ref[...])
        g = g_ref[0, rows, :].astype(F32)
        o_ref[0, rows, :] = (y * (g * jax.nn.sigmoid(g))).astype(BF16)


def _gla(q, k, v, g, dec, p):
    B, L, _ = q.shape
    C = GLA_CHUNK
    head = lambda n: pl.BlockSpec((1, L, n), lambda b, h: (b, 0, h))
    wspec = lambda a: pl.BlockSpec((a.shape[0], GLA_DK), lambda b, h: (0, h))
    return pl.pallas_call(
        _gla_kernel,
        grid=(B, GLA_HEADS),
        in_specs=[head(GLA_DK), head(GLA_DK), head(GLA_DV), head(GLA_DV),
                  pl.BlockSpec((1, L, 2 * GLA_RANK), lambda b, h: (b, 0, 0)),
                  wspec(p['wa_f']), wspec(p['ba_f']), wspec(p['wa_b']), wspec(p['ba_b']),
                  pl.BlockSpec((1, GLA_DV), lambda b, h: (0, 0))],
        out_specs=head(GLA_DV),
        out_shape=jax.ShapeDtypeStruct((B, L, GLA_HEADS * GLA_DV), BF16),
        scratch_shapes=[pltpu.VMEM((C, C), BF16), pltpu.VMEM((C, C), BF16),
                        pltpu.VMEM((GLA_DK, GLA_DV), F32), pltpu.VMEM((GLA_DK, GLA_DV), F32),
                        pltpu.VMEM((L, 2 * GLA_DK), BF16), pltpu.VMEM((L, 2 * GLA_DK), BF16),
                        pltpu.VMEM((L, GLA_DK), F32), pltpu.VMEM((L, GLA_DK), F32),
                        pltpu.VMEM((L, GLA_DK), BF16), pltpu.VMEM((L, GLA_DK), BF16),
                        pltpu.VMEM((L, GLA_DV), F32), pltpu.VMEM((L, GLA_DV), F32),
                        pltpu.VMEM((L, GLA_DK), F32), pltpu.VMEM((L, GLA_DK), F32),
                        pltpu.VMEM((L, C), BF16), pltpu.VMEM((L, GLA_DV), F32)],
        compiler_params=_params(),
        name="gla",
    )(q, k, v, g, dec, p['wa_f'], p['ba_f'], p['wa_b'], p['ba_b'], p['gla_norm'])


def _mla_kernel(cq_ref, ckv_ref, kpe_ref, rope_ref, wq_ref, wkv_ref, o_ref,
                qs_ref, ks_ref, vt_ref, s0_ref, s1_ref, m0_ref, m1_ref, acc_ref):
    L = cq_ref.shape[1]
    scale = (MLA_NOPE + MLA_ROPE) ** -0.5 * LOG2_E
    qall = _dot(cq_ref[0], wq_ref[0])
    qs_ref[:, :MLA_NOPE] = (qall[:, :MLA_NOPE] * scale).astype(BF16)
    qs_ref[:, MLA_NOPE:] = (_rope(qall[:, MLA_NOPE:], rope_ref[...]) * scale).astype(BF16)
    kv = _dot(ckv_ref[0], wkv_ref[0])
    ks_ref[:, :MLA_NOPE] = kv[:, :MLA_NOPE].astype(BF16)
    ks_ref[:, MLA_NOPE:] = kpe_ref[0]
    vt_ref[:MLA_V, :] = kv[:, MLA_NOPE:].T.astype(BF16)
    vt_ref[MLA_V:, :] = jnp.ones((ONES_ROWS, L), BF16)

    tq = min(Q_TOKENS, L)
    nq = L // tq
    kc = min(KEY_CHUNK, L)
    assert nq % 2 == 0

    def tile(t):
        start = t * tq
        return pl.ds(start if isinstance(t, int) else pl.multiple_of(start, tq), tq)

    def step(t_val, t_sc, rd, wr):
        if t_val is not None:
            s_rd, m_rd = rd
            m = jnp.max(m_rd[...], axis=0, keepdims=True)
        if t_sc is not None:
            s_wr, m_wr = wr
            q_sc = qs_ref[tile(t_sc), :]
        acc = None
        mx = None
        for c in range(L // kc):
            keys = slice(c * kc, (c + 1) * kc)
            if t_val is not None:
                e = jnp.exp2(s_rd[keys, :] - m).astype(BF16)
                part = _dot(vt_ref[:, keys], e)
                acc = part if acc is None else acc + part
            if t_sc is not None:
                s = _dot_nt(ks_ref[keys, :], q_sc)
                s_wr[keys, :] = s
                cm = jnp.max(s.reshape(kc // 8, 8, tq), axis=0)
                mx = cm if mx is None else jnp.maximum(mx, cm)
        if t_sc is not None:
            m_wr[...] = mx
        if t_val is not None:
            acc_ref[...] = acc

    def finish(t):
        acc = acc_ref[...]
        o_t = acc[:MLA_V] * (1.0 / acc[MLA_V:MLA_V + 1])
        o_ref[0, tile(t), :] = o_t.T.astype(BF16)

    buf0, buf1 = (s0_ref, m0_ref), (s1_ref, m1_ref)
    step(None, 0, None, buf0)
    step(0, 1, buf0, buf1)

    def two_steps(j, carry):
        t = 2 * j + 1
        finish(t - 1)
        step(t, t + 1, buf1, buf0)
        finish(t)
        step(t + 1, t + 2, buf0, buf1)
        return carry

    lax.fori_loop(0, nq // 2 - 1, two_steps, 0)
    finish(nq - 2)
    step(nq - 1, None, buf1, None)
    finish(nq - 1)


def _mla(cq, ckv, kpe, p):
    B, L, R = cq.shape
    dqk = MLA_NOPE + MLA_ROPE
    tq = min(Q_TOKENS, L)
    seq = lambda n: pl.BlockSpec((1, L, n), lambda b, h: (b, 0, 0))
    per_head = lambda a: pl.BlockSpec((1,) + a.shape[1:], lambda b, h: (h, 0, 0))
    return pl.pallas_call(
        _mla_kernel,
        grid=(B, MLA_HEADS),
        in_specs=[seq(R), seq(ckv.shape[2]), seq(MLA_ROPE), _resident(p['rope'].shape),
                  per_head(p['wq_mla']), per_head(p['wkv_mla'])],
        out_specs=pl.BlockSpec((1, L, MLA_V), lambda b, h: (b, 0, h)),
        out_shape=jax.ShapeDtypeStruct((B, L, MLA_HEADS * MLA_V), BF16),
        scratch_shapes=[pltpu.VMEM((L, dqk), BF16), pltpu.VMEM((L, dqk), BF16),
                        pltpu.VMEM((MLA_V + ONES_ROWS, L), BF16),
                        pltpu.VMEM((L, tq), F32), pltpu.VMEM((L, tq), F32),
                        pltpu.VMEM((8, tq), F32), pltpu.VMEM((8, tq), F32),
                        pltpu.VMEM((MLA_V + ONES_ROWS, tq), F32)],
        compiler_params=_params(),
        name="mla",
    )(cq, ckv, kpe, p['rope'], p['wq_mla'], p['wkv_mla'])


def _out_kernel(x_ref, oa_ref, ob_ref, ga_ref, gb_ref, woa_ref, wob_ref, wout_ref,
                wgate_ref, wup_ref, wdown_ref, npost_ref, nfpre_ref, nfpost_ref, y_ref):
    tm = x_ref.shape[1]
    parts = [pl.ds(i * (tm // OUT_PARTS), tm // OUT_PARTS) for i in range(OUT_PARTS)]
    ya = [_dot(oa_ref[0, r, :], woa_ref[...]) for r in parts]
    yb = [_dot(ob_ref[0, r, :], wob_ref[...]) for r in parts]
    merged = [(ga_ref[0, r, :].astype(F32) * a + gb_ref[0, r, :].astype(F32) * b).astype(BF16)
              for r, a, b in zip(parts, ya, yb)]
    mo = [_dot(m, wout_ref[...]) for m in merged]
    x1 = [x_ref[0, r, :] + _rms(o, npost_ref[...]) for r, o in zip(parts, mo)]
    h = [_rms(x, nfpre_ref[...]).astype(BF16) for x in x1]
    gate = [_dot(v, wgate_ref[...]) for v in h]
    up = [_dot(v, wup_ref[...]) for v in h]
    act = [(g * jax.nn.sigmoid(g) * u).astype(BF16) for g, u in zip(gate, up)]
    down = [_dot(a, wdown_ref[...]) for a in act]
    for r, x, d in zip(parts, x1, down):
        y_ref[0, r, :] = x + _rms(d, nfpost_ref[...])


def _out(x, oa, ob, ga, gb, p):
    B, L, D = x.shape
    tm = min(OUT_TOKENS, L)
    tok = pl.BlockSpec((1, tm, D), lambda b, j: (b, j, 0))
    consts = [p['w_o_gla'], p['w_o_mla'], p['w_out'], p['w_gate'], p['w_up'], p['w_down'],
              p['g_post'], p['g_ffn_pre'], p['g_ffn_post']]
    return pl.pallas_call(
        _out_kernel,
        grid=(B, L // tm),
        in_specs=[tok] * 5 + [_resident(c.shape) for c in consts],
        out_specs=tok,
        out_shape=jax.ShapeDtypeStruct((B, L, D), F32),
        compiler_params=_params(),
        name="merge_ffn",
    )(x, oa, ob, ga, gb, *consts)


def _prepare(L, norm_mix_pre, w_in, gla_wa_fwd, gla_ba_fwd, gla_wa_bwd, gla_ba_bwd, gla_norm,
             w_o_gla, mla_norm_q, w_uq, mla_norm_kv, w_uk, w_uv, w_o_mla, w_out,
             norm_mix_post, norm_ffn_pre, w_gate, w_up, w_down, norm_ffn_post):
    D = w_in.shape[0]
    qk = GLA_HEADS * GLA_DK
    vw = GLA_HEADS * GLA_DV
    rq, rkv = mla_norm_q.shape[0], mla_norm_kv.shape[0]
    widths = (qk, qk, vw, vw, GLA_RANK, GLA_RANK, rq, rkv, MLA_ROPE, D, D)
    offs = np.concatenate([[0], np.cumsum(widths)]).tolist()
    piece = lambda i, j=None: w_in[:, offs[i]:offs[(i if j is None else j) + 1]]
    swap = np.concatenate([np.arange(MLA_ROPE // 2, MLA_ROPE), np.arange(MLA_ROPE // 2)])
    w_kr = piece(8)
    row = lambda a: a.reshape(1, -1).astype(F32)

    inv = ROPE_BASE ** (-jnp.arange(0, MLA_ROPE, 2, dtype=F32) / MLA_ROPE)
    ang = jnp.arange(L, dtype=F32)[:, None] * inv[None, :]
    cos, sin = jnp.cos(ang), jnp.sin(ang)

    uq = w_uq.reshape(rq, MLA_HEADS, MLA_NOPE + MLA_ROPE)
    wq_pe = uq[:, :, MLA_NOPE:]
    wq_mla = jnp.concatenate([uq[:, :, :MLA_NOPE], wq_pe, wq_pe[:, :, swap]], axis=2)
    wkv_mla = jnp.concatenate([w_uk.reshape(rkv, MLA_HEADS, MLA_NOPE),
                               w_uv.reshape(rkv, MLA_HEADS, MLA_V)], axis=2)
    zeros = jnp.zeros((GLA_RANK, qk), F32)
    return dict(
        g_pre=row(norm_mix_pre),
        wq=piece(0).astype(BF16), wk=piece(1).astype(BF16), wv=piece(2).astype(BF16),
        wg=piece(3).astype(BF16), wlat=piece(6, 7).astype(BF16),
        wmisc=jnp.concatenate([w_kr, w_kr[:, swap], piece(4, 5)], axis=1).astype(BF16),
        wga=piece(9).astype(BF16), wgb=piece(10).astype(BF16),
        nq=row(mla_norm_q), nkv=row(mla_norm_kv),
        rope=jnp.concatenate([cos, cos, -sin, sin], axis=1),
        wa_f=jnp.concatenate([gla_wa_fwd, zeros], axis=0).astype(BF16),
        wa_b=jnp.concatenate([zeros, gla_wa_bwd], axis=0).astype(BF16),
        ba_f=row(gla_ba_fwd), ba_b=row(gla_ba_bwd), gla_norm=row(gla_norm),
        wq_mla=jnp.transpose(wq_mla, (1, 0, 2)).astype(BF16),
        wkv_mla=jnp.transpose(wkv_mla, (1, 0, 2)).astype(BF16),
        w_o_gla=w_o_gla.astype(BF16), w_o_mla=w_o_mla.astype(BF16), w_out=w_out.astype(BF16),
        w_gate=w_gate.astype(BF16), w_up=w_up.astype(BF16), w_down=w_down.astype(BF16),
        g_post=row(norm_mix_post), g_ffn_pre=row(norm_ffn_pre), g_ffn_post=row(norm_ffn_post),
    )


def _layer(x, p):
    q, k, v, g, dec, cq, ckv, kpe, ga, gb = _in_proj(x, p)
    oa = _gla(q, k, v, g, dec, p)
    ob = _mla(cq, ckv, kpe, p)
    return _out(x, oa, ob, ga, gb, p)


def kernel(x_prompt, x_sample, norm_mix_pre, w_in, gla_wa_fwd, gla_ba_fwd, gla_wa_bwd, gla_ba_bwd, gla_norm, w_o_gla, mla_norm_q, w_uq, mla_norm_kv, w_uk, w_uv, w_o_mla, w_out, norm_mix_post, norm_ffn_pre, w_gate, w_up, w_down, norm_ffn_post):
    assert x_prompt.shape[1] == x_sample.shape[1]
    stacked = (norm_mix_pre, w_in, gla_wa_fwd, gla_ba_fwd, gla_wa_bwd, gla_ba_bwd, gla_norm,
               w_o_gla, mla_norm_q, w_uq, mla_norm_kv, w_uk, w_uv, w_o_mla, w_out,
               norm_mix_post, norm_ffn_pre, w_gate, w_up, w_down, norm_ffn_post)
    y_prompt, y_sample = x_prompt, x_sample
    for layer in range(norm_mix_pre.shape[0]):
        p = _prepare(x_prompt.shape[1], *[a[layer] for a in stacked])
        y_prompt, y_sample = _layer(y_prompt, p), _layer(y_sample, p)
    return (y_prompt, y_sample)
```

```python
import jax
import jax.numpy as jnp
import numpy as np
from jax import lax
from jax.experimental import pallas as pl
from jax.experimental.pallas import tpu as pltpu

F32 = jnp.float32
BF16 = jnp.bfloat16

EPS = 1e-6
GLA_HEADS = 4
GLA_DK = 128
GLA_DV = 256
GLA_RANK = 16
GLA_TEMP = 16.0
GLA_CHUNK = 128
MLA_HEADS = 8
MLA_NOPE = 128
MLA_ROPE = 64
MLA_V = 128
ROPE_BASE = 10000.0

IN_TOKENS = 512
OUT_TOKENS = 512
OUT_PARTS = 2
Q_TOKENS = 256
LOG2_E = 1.4426950408889634
GLA_UNROLL = 4
NORM_ROWS = 256
VMEM_LIMIT_BYTES = 56 * 1024 * 1024


def _rms(xf, g):
    ms = jnp.mean(xf * xf, axis=-1, keepdims=True)
    return xf * lax.rsqrt(ms + EPS) * g


def _dot(a, b):
    return jnp.dot(a, b, preferred_element_type=F32)


def _dot_nt(a, b):
    return lax.dot_general(a, b, (((1,), (1,)), ((), ())), preferred_element_type=F32)


def _rope(x2, table):
    y = x2 * table
    return (y + pltpu.roll(y, MLA_ROPE, axis=1))[:, :MLA_ROPE]


def _params():
    return pltpu.CompilerParams(
        dimension_semantics=("arbitrary", "arbitrary"),
        vmem_limit_bytes=VMEM_LIMIT_BYTES,
    )


def _resident(shape):
    nd = len(shape)
    return pl.BlockSpec(shape, lambda *_: (0,) * nd, pipeline_mode=pl.Buffered(1))


def _in_proj_kernel(x_ref, g_ref, rope_ref, nq_ref, nkv_ref,
                    wq_ref, wk_ref, wv_ref, wg_ref, wlat_ref, wmisc_ref, wga_ref, wgb_ref,
                    wa_ref, ba_ref,
                    q_out, k_out, v_out, g_out, laf_out, lab_out, cq_out, ckv_out, kpe_out,
                    ga_out, gb_out):
    h = _rms(x_ref[0], g_ref[...]).astype(BF16)

    misc = _dot(h, wmisc_ref[...])
    kpe_out[0] = _rope(misc[:, :2 * MLA_ROPE], rope_ref[...]).astype(BF16)

    z = _dot(misc[:, 2 * MLA_ROPE:2 * MLA_ROPE + 2 * GLA_RANK].astype(BF16), wa_ref[...]) + ba_ref[...]
    la = (jnp.minimum(z, 0.0) - jnp.log(1.0 + jnp.exp(-jnp.abs(z)))) * (LOG2_E / GLA_TEMP)
    qk = GLA_HEADS * GLA_DK
    laf_out[0] = la[:, :qk]
    lab_out[0] = la[:, qk:]

    v_out[0] = _dot(h, wv_ref[...]).astype(BF16)
    g_out[0] = _dot(h, wg_ref[...]).astype(BF16)
    ga_out[0] = jax.nn.sigmoid(_dot(h, wga_ref[...])).astype(BF16)
    q_out[0] = _dot(h, wq_ref[...]).astype(BF16)
    gb_out[0] = jax.nn.sigmoid(_dot(h, wgb_ref[...])).astype(BF16)

    lat = _dot(h, wlat_ref[...])
    r = nq_ref.shape[1]
    cq_out[0] = _rms(lat[:, :r], nq_ref[...]).astype(BF16)
    ckv_out[0] = _rms(lat[:, r:], nkv_ref[...]).astype(BF16)
    k_out[0] = _dot(h, wk_ref[...]).astype(BF16)


def _in_proj(x, p):
    B, L, D = x.shape
    tm = min(IN_TOKENS, L)
    grid = (B, L // tm)
    tok = lambda n, dt=BF16: (pl.BlockSpec((1, tm, n), lambda b, j: (b, j, 0)),
                              jax.ShapeDtypeStruct((B, L, n), dt))
    outs = [tok(p['wq'].shape[1]), tok(p['wk'].shape[1]), tok(p['wv'].shape[1]),
            tok(p['wg'].shape[1]), tok(p['wq'].shape[1], F32), tok(p['wq'].shape[1], F32),
            tok(p['nq'].shape[1]), tok(p['nkv'].shape[1]), tok(MLA_ROPE), tok(D), tok(D)]
    weights = [p['wq'], p['wk'], p['wv'], p['wg'], p['wlat'], p['wmisc'], p['wga'], p['wgb'],
               p['wa'], p['ba']]
    return pl.pallas_call(
        _in_proj_kernel,
        grid=grid,
        in_specs=[pl.BlockSpec((1, tm, D), lambda b, j: (b, j, 0)),
                  _resident(p['g_pre'].shape),
                  pl.BlockSpec((tm, 2 * MLA_ROPE), lambda b, j: (j, 0)),
                  _resident(p['nq'].shape), _resident(p['nkv'].shape)]
                 + [_resident(w.shape) for w in weights],
        out_specs=[o[0] for o in outs],
        out_shape=[o[1] for o in outs],
        compiler_params=_params(),
        name="in_proj",
    )(x, p['g_pre'], p['rope'], p['nq'], p['nkv'], *weights)


def _gla_kernel(q_ref, k_ref, v_ref, g_ref, laf_ref, lab_ref,
                gn_ref, o_ref, mf_ref, mb_ref, sf_ref, sb_ref, bf_ref, bb_ref,
                qif_ref, qib_ref, uf_ref, ub_ref, dtf_ref, dtb_ref, a_ref, oacc_ref):
    L = q_ref.shape[1]
    C = GLA_CHUNK
    nc = L // C
    nr = min(NORM_ROWS, L)
    mid = C // 2 - 1
    scale = GLA_DK ** -0.5

    row = lax.broadcasted_iota(jnp.int32, (C, C), 0)
    col = lax.broadcasted_iota(jnp.int32, (C, C), 1)
    mf_ref[...] = jnp.where(col <= row, 1.0, 0.0).astype(BF16)
    mb_ref[...] = jnp.where(col >= row, 1.0, 0.0).astype(BF16)

    def cumsum(c, carry):
        rows = pl.ds(pl.multiple_of(c * C, C), C)
        for m_ref, la_ref, b_ref in ((mf_ref, laf_ref, bf_ref), (mb_ref, lab_ref, bb_ref)):
            la = la_ref[0, rows, :]
            hi = la.astype(BF16)
            lo = (la - hi.astype(F32)).astype(BF16)
            cs = _dot(m_ref[...], jnp.concatenate([hi, lo], axis=1))
            b_ref[rows, :] = cs[:, :GLA_DK] + cs[:, GLA_DK:]
        return carry

    lax.fori_loop(0, nc, cumsum, 0, unroll=GLA_UNROLL)

    dirs = ((bf_ref, qif_ref, uf_ref, dtf_ref, C - 1, mid, True),
            (bb_ref, qib_ref, ub_ref, dtb_ref, 0, mid + 1, False))

    def intra(c, carry):
        rows = pl.ds(pl.multiple_of(c * C, C), C)
        q = q_ref[0, rows, :].astype(F32) * scale
        k = k_ref[0, rows, :].astype(F32)
        v = v_ref[0, rows, :]
        a_sum = None
        for b_ref, qi_ref, u_ref, dt_ref, tot_row, ref_row, fwd in dirs:
            b_incl = b_ref[rows, :]
            b_tot = b_incl[tot_row:tot_row + 1, :]
            b_rel = b_incl - b_incl[ref_row:ref_row + 1, :]
            qi_ref[rows, :] = (q * jnp.exp2(b_incl)).astype(BF16)
            q_intra = (q * jnp.exp2(b_rel)).astype(BF16)
            k_intra = (k * jnp.exp2(-b_rel)).astype(BF16)
            k_end = k * jnp.exp2(b_tot - b_incl)
            mask = (row >= col) if fwd else (row <= col)
            a = jnp.where(mask, _dot_nt(q_intra, k_intra), 0.0)
            a_sum = a if a_sum is None else a_sum + a
            u_ref[rows, :] = _dot(k_end.T.astype(BF16), v)
            dt_ref[rows, :] = jnp.broadcast_to(jnp.exp2(b_tot), (C, GLA_DK)).T
        a_ref[rows, :] = a_sum.astype(BF16)
        return carry

    lax.fori_loop(0, nc, intra, 0, unroll=2)

    def intra_out(c, carry):
        rows = pl.ds(pl.multiple_of(c * C, C), C)
        oacc_ref[rows, :] = _dot(a_ref[rows, :], v_ref[0, rows, :])
        return carry

    lax.fori_loop(0, nc, intra_out, 0, unroll=GLA_UNROLL)

    sf_ref[...] = jnp.zeros_like(sf_ref)
    sb_ref[...] = jnp.zeros_like(sb_ref)

    def scan(i, carry):
        for s_ref, qi_ref, u_ref, dt_ref, c in ((sf_ref, qif_ref, uf_ref, dtf_ref, i),
                                                (sb_ref, qib_ref, ub_ref, dtb_ref, nc - 1 - i)):
            rows = pl.ds(pl.multiple_of(c * C, C), C)
            s = s_ref[...]
            oacc_ref[rows, :] += _dot(qi_ref[rows, :], s.astype(BF16))
            dt = dt_ref[rows, :]
            s_ref[...] = s * jnp.concatenate([dt] * (GLA_DV // C), axis=1) + u_ref[rows, :]
        return carry

    lax.fori_loop(0, nc, scan, 0, unroll=GLA_UNROLL)

    for r in range(L // nr):
        rows = pl.ds(r * nr, nr)
        y = _rms(oacc_ref[rows, :], gn_ref[...])
        g = g_ref[0, rows, :].astype(F32)
        o_ref[0, rows, :] = (y * (g * jax.nn.sigmoid(g))).astype(BF16)


def _gla(q, k, v, g, la_f, la_b, p):
    B, L, _ = q.shape
    C = GLA_CHUNK
    head = lambda n: pl.BlockSpec((1, L, n), lambda b, h: (b, 0, h))
    return pl.pallas_call(
        _gla_kernel,
        grid=(B, GLA_HEADS),
        in_specs=[head(GLA_DK), head(GLA_DK), head(GLA_DV), head(GLA_DV),
                  head(GLA_DK), head(GLA_DK),
                  pl.BlockSpec((1, GLA_DV), lambda b, h: (0, 0))],
        out_specs=head(GLA_DV),
        out_shape=jax.ShapeDtypeStruct((B, L, GLA_HEADS * GLA_DV), BF16),
        scratch_shapes=[pltpu.VMEM((C, C), BF16), pltpu.VMEM((C, C), BF16),
                        pltpu.VMEM((GLA_DK, GLA_DV), F32), pltpu.VMEM((GLA_DK, GLA_DV), F32),
                        pltpu.VMEM((L, GLA_DK), F32), pltpu.VMEM((L, GLA_DK), F32),
                        pltpu.VMEM((L, GLA_DK), BF16), pltpu.VMEM((L, GLA_DK), BF16),
                        pltpu.VMEM((L, GLA_DV), F32), pltpu.VMEM((L, GLA_DV), F32),
                        pltpu.VMEM((L, GLA_DK), F32), pltpu.VMEM((L, GLA_DK), F32),
                        pltpu.VMEM((L, C), BF16), pltpu.VMEM((L, GLA_DV), F32)],
        compiler_params=_params(),
        name="gla",
    )(q, k, v, g, la_f, la_b, p['gla_norm'])


def _mla_kernel(cq_ref, ckv_ref, kpe_ref, rope_ref, wq_ref, wkv_ref, o_ref,
                qs_ref, ks_ref, vs_ref):
    L = cq_ref.shape[1]
    scale = (MLA_NOPE + MLA_ROPE) ** -0.5 * LOG2_E
    qall = _dot(cq_ref[0], wq_ref[0])
    qs_ref[:, :MLA_NOPE] = (qall[:, :MLA_NOPE] * scale).astype(BF16)
    qs_ref[:, MLA_NOPE:] = (_rope(qall[:, MLA_NOPE:], rope_ref[...]) * scale).astype(BF16)
    kv = _dot(ckv_ref[0], wkv_ref[0])
    ks_ref[:, :MLA_NOPE] = kv[:, :MLA_NOPE].astype(BF16)
    ks_ref[:, MLA_NOPE:] = kpe_ref[0]
    vs_ref[...] = kv[:, MLA_NOPE:].astype(BF16)

    tq = min(Q_TOKENS, L)
    nq = L // tq
    scores = lambda i: _dot_nt(qs_ref[i * tq:(i + 1) * tq, :], ks_ref[...])
    s = scores(0)
    for i in range(nq):
        s_next = scores(i + 1) if i + 1 < nq else None
        e = jnp.exp2(s - jnp.max(s, axis=-1, keepdims=True))
        l = jnp.sum(e, axis=-1, keepdims=True)
        o = _dot(e.astype(BF16), vs_ref[...]) * (1.0 / l)
        o_ref[0, i * tq:(i + 1) * tq, :] = o.astype(BF16)
        s = s_next


def _mla(cq, ckv, kpe, p):
    B, L, R = cq.shape
    dqk = MLA_NOPE + MLA_ROPE
    seq = lambda n: pl.BlockSpec((1, L, n), lambda b, h: (b, 0, 0))
    per_head = lambda a: pl.BlockSpec((1,) + a.shape[1:], lambda b, h: (h, 0, 0))
    return pl.pallas_call(
        _mla_kernel,
        grid=(B, MLA_HEADS),
        in_specs=[seq(R), seq(ckv.shape[2]), seq(MLA_ROPE), _resident(p['rope'].shape),
                  per_head(p['wq_mla']), per_head(p['wkv_mla'])],
        out_specs=pl.BlockSpec((1, L, MLA_V), lambda b, h: (b, 0, h)),
        out_shape=jax.ShapeDtypeStruct((B, L, MLA_HEADS * MLA_V), BF16),
        scratch_shapes=[pltpu.VMEM((L, dqk), BF16), pltpu.VMEM((L, dqk), BF16),
                        pltpu.VMEM((L, MLA_V), BF16)],
        compiler_params=_params(),
        name="mla",
    )(cq, ckv, kpe, p['rope'], p['wq_mla'], p['wkv_mla'])


def _out_kernel(x_ref, oa_ref, ob_ref, ga_ref, gb_ref, woa_ref, wob_ref, wout_ref,
                wgate_ref, wup_ref, wdown_ref, npost_ref, nfpre_ref, nfpost_ref, y_ref):
    tm = x_ref.shape[1]
    parts = [pl.ds(i * (tm // OUT_PARTS), tm // OUT_PARTS) for i in range(OUT_PARTS)]
    ya = [_dot(oa_ref[0, r, :], woa_ref[...]) for r in parts]
    yb = [_dot(ob_ref[0, r, :], wob_ref[...]) for r in parts]
    merged = [(ga_ref[0, r, :].astype(F32) * a + gb_ref[0, r, :].astype(F32) * b).astype(BF16)
              for r, a, b in zip(parts, ya, yb)]
    mo = [_dot(m, wout_ref[...]) for m in merged]
    x1 = [x_ref[0, r, :] + _rms(o, npost_ref[...]) for r, o in zip(parts, mo)]
    h = [_rms(x, nfpre_ref[...]).astype(BF16) for x in x1]
    gate = [_dot(v, wgate_ref[...]) for v in h]
    up = [_dot(v, wup_ref[...]) for v in h]
    act = [(g * jax.nn.sigmoid(g) * u).astype(BF16) for g, u in zip(gate, up)]
    down = [_dot(a, wdown_ref[...]) for a in act]
    for r, x, d in zip(parts, x1, down):
        y_ref[0, r, :] = x + _rms(d, nfpost_ref[...])


def _out(x, oa, ob, ga, gb, p):
    B, L, D = x.shape
    tm = min(OUT_TOKENS, L)
    tok = pl.BlockSpec((1, tm, D), lambda b, j: (b, j, 0))
    consts = [p['w_o_gla'], p['w_o_mla'], p['w_out'], p['w_gate'], p['w_up'], p['w_down'],
              p['g_post'], p['g_ffn_pre'], p['g_ffn_post']]
    return pl.pallas_call(
        _out_kernel,
        grid=(B, L // tm),
        in_specs=[tok] * 5 + [_resident(c.shape) for c in consts],
        out_specs=tok,
        out_shape=jax.ShapeDtypeStruct((B, L, D), F32),
        compiler_params=_params(),
        name="merge_ffn",
    )(x, oa, ob, ga, gb, *consts)


def _prepare(L, norm_mix_pre, w_in, gla_wa_fwd, gla_ba_fwd, gla_wa_bwd, gla_ba_bwd, gla_norm,
             w_o_gla, mla_norm_q, w_uq, mla_norm_kv, w_uk, w_uv, w_o_mla, w_out,
             norm_mix_post, norm_ffn_pre, w_gate, w_up, w_down, norm_ffn_post):
    D = w_in.shape[0]
    qk = GLA_HEADS * GLA_DK
    vw = GLA_HEADS * GLA_DV
    rq, rkv = mla_norm_q.shape[0], mla_norm_kv.shape[0]
    widths = (qk, qk, vw, vw, GLA_RANK, GLA_RANK, rq, rkv, MLA_ROPE, D, D)
    offs = np.concatenate([[0], np.cumsum(widths)]).tolist()
    piece = lambda i, j=None: w_in[:, offs[i]:offs[(i if j is None else j) + 1]]
    swap = np.concatenate([np.arange(MLA_ROPE // 2, MLA_ROPE), np.arange(MLA_ROPE // 2)])
    w_kr = piece(8)
    row = lambda a: a.reshape(1, -1).astype(F32)

    inv = ROPE_BASE ** (-jnp.arange(0, MLA_ROPE, 2, dtype=F32) / MLA_ROPE)
    ang = jnp.arange(L, dtype=F32)[:, None] * inv[None, :]
    cos, sin = jnp.cos(ang), jnp.sin(ang)

    uq = w_uq.reshape(rq, MLA_HEADS, MLA_NOPE + MLA_ROPE)
    wq_pe = uq[:, :, MLA_NOPE:]
    wq_mla = jnp.concatenate([uq[:, :, :MLA_NOPE], wq_pe, wq_pe[:, :, swap]], axis=2)
    wkv_mla = jnp.concatenate([w_uk.reshape(rkv, MLA_HEADS, MLA_NOPE),
                               w_uv.reshape(rkv, MLA_HEADS, MLA_V)], axis=2)
    zeros = jnp.zeros((GLA_RANK, qk), F32)
    return dict(
        g_pre=row(norm_mix_pre),
        wq=piece(0).astype(BF16), wk=piece(1).astype(BF16), wv=piece(2).astype(BF16),
        wg=piece(3).astype(BF16), wlat=piece(6, 7).astype(BF16),
        wmisc=jnp.concatenate([w_kr, w_kr[:, swap], piece(4, 5)], axis=1).astype(BF16),
        wga=piece(9).astype(BF16), wgb=piece(10).astype(BF16),
        nq=row(mla_norm_q), nkv=row(mla_norm_kv),
        rope=jnp.concatenate([cos, cos, -sin, sin], axis=1),
        wa=jnp.concatenate([jnp.concatenate([gla_wa_fwd, zeros], axis=1),
                            jnp.concatenate([zeros, gla_wa_bwd], axis=1)], axis=0).astype(BF16),
        ba=jnp.concatenate([row(gla_ba_fwd), row(gla_ba_bwd)], axis=1), gla_norm=row(gla_norm),
        wq_mla=jnp.transpose(wq_mla, (1, 0, 2)).astype(BF16),
        wkv_mla=jnp.transpose(wkv_mla, (1, 0, 2)).astype(BF16),
        w_o_gla=w_o_gla.astype(BF16), w_o_mla=w_o_mla.astype(BF16), w_out=w_out.astype(BF16),
        w_gate=w_gate.astype(BF16), w_up=w_up.astype(BF16), w_down=w_down.astype(BF16),
        g_post=row(norm_mix_post), g_ffn_pre=row(norm_ffn_pre), g_ffn_post=row(norm_ffn_post),
    )


def _layer(x, p):
    q, k, v, g, la_f, la_b, cq, ckv, kpe, ga, gb = _in_proj(x, p)
    oa = _gla(q, k, v, g, la_f, la_b, p)
    ob = _mla(cq, ckv, kpe, p)
    return _out(x, oa, ob, ga, gb, p)


def kernel(x_prompt, x_sample, norm_mix_pre, w_in, gla_wa_fwd, gla_ba_fwd, gla_wa_bwd, gla_ba_bwd, gla_norm, w_o_gla, mla_norm_q, w_uq, mla_norm_kv, w_uk, w_uv, w_o_mla, w_out, norm_mix_post, norm_ffn_pre, w_gate, w_up, w_down, norm_ffn_post):
    assert x_prompt.shape[1] == x_sample.shape[1]
    stacked = (norm_mix_pre, w_in, gla_wa_fwd, gla_ba_fwd, gla_wa_bwd, gla_ba_bwd, gla_norm,
               w_o_gla, mla_norm_q, w_uq, mla_norm_kv, w_uk, w_uv, w_o_mla, w_out,
               norm_mix_post, norm_ffn_pre, w_gate, w_up, w_down, norm_ffn_post)
    y_prompt, y_sample = x_prompt, x_sample
    for layer in range(norm_mix_pre.shape[0]):
        p = _prepare(x_prompt.shape[1], *[a[layer] for a in stacked])
        y_prompt, y_sample = _layer(y_prompt, p), _layer(y_sample, p)
    return (y_prompt, y_sample)
```

```python
import jax
import jax.numpy as jnp
import numpy as np
from jax import lax
from jax.experimental import pallas as pl
from jax.experimental.pallas import tpu as pltpu

F32 = jnp.float32
BF16 = jnp.bfloat16

EPS = 1e-6
GLA_HEADS = 4
GLA_DK = 128
GLA_DV = 256
GLA_RANK = 16
GLA_TEMP = 16.0
GLA_CHUNK = 128
MLA_HEADS = 8
MLA_NOPE = 128
MLA_ROPE = 64
MLA_V = 128
ROPE_BASE = 10000.0

IN_TOKENS = 512
IN_PARTS = 2
OUT_TOKENS = 512
OUT_PARTS = 2
Q_TOKENS = 512
LOG2_E = 1.4426950408889634
GLA_UNROLL = 4
NORM_ROWS = 256
VMEM_LIMIT_BYTES = 56 * 1024 * 1024


def _rms(xf, g):
    ms = jnp.mean(xf * xf, axis=-1, keepdims=True)
    return xf * lax.rsqrt(ms + EPS) * g


def _dot(a, b):
    return jnp.dot(a, b, preferred_element_type=F32)


def _dot_nt(a, b):
    return lax.dot_general(a, b, (((1,), (1,)), ((), ())), preferred_element_type=F32)


def _rope(x2, table):
    y = x2 * table
    return (y + pltpu.roll(y, MLA_ROPE, axis=1))[:, :MLA_ROPE]


def _params():
    return pltpu.CompilerParams(
        dimension_semantics=("arbitrary", "arbitrary"),
        vmem_limit_bytes=VMEM_LIMIT_BYTES,
    )


def _resident(shape):
    nd = len(shape)
    return pl.BlockSpec(shape, lambda *_: (0,) * nd, pipeline_mode=pl.Buffered(1))


def _in_proj_kernel(x_ref, g_ref, rope_ref, nq_ref, nkv_ref,
                    wq_ref, wk_ref, wv_ref, wg_ref, wlat_ref, wmisc_ref, wga_ref, wgb_ref,
                    wa_ref, ba_ref,
                    q_out, k_out, v_out, g_out, laf_out, lab_out, cq_out, ckv_out, kpe_out,
                    ga_out, gb_out):
    tm = x_ref.shape[1]
    parts = [pl.ds(i * (tm // IN_PARTS), tm // IN_PARTS) for i in range(IN_PARTS)]
    h = [_rms(x_ref[0, r, :], g_ref[...]).astype(BF16) for r in parts]

    def project(w_ref, out, epilogue=lambda y: y):
        for r, y in zip(parts, [_dot(v, w_ref[...]) for v in h]):
            out[0, r, :] = epilogue(y).astype(out.dtype)

    misc = [_dot(v, wmisc_ref[...]) for v in h]
    for r, m in zip(parts, misc):
        kpe_out[0, r, :] = _rope(m[:, :2 * MLA_ROPE], rope_ref[r, :]).astype(BF16)

    lat_cols = slice(2 * MLA_ROPE, 2 * MLA_ROPE + 2 * GLA_RANK)
    zs = [_dot(m[:, lat_cols].astype(BF16), wa_ref[...]) + ba_ref[...] for m in misc]
    qk = GLA_HEADS * GLA_DK
    for r, z in zip(parts, zs):
        la = (jnp.minimum(z, 0.0) - jnp.log(1.0 + jnp.exp(-jnp.abs(z)))) * (LOG2_E / GLA_TEMP)
        laf_out[0, r, :] = la[:, :qk]
        lab_out[0, r, :] = la[:, qk:]

    project(wv_ref, v_out)
    project(wg_ref, g_out)
    project(wga_ref, ga_out, jax.nn.sigmoid)
    project(wq_ref, q_out)
    project(wgb_ref, gb_out, jax.nn.sigmoid)
    rq = nq_ref.shape[1]
    for r, lat in zip(parts, [_dot(v, wlat_ref[...]) for v in h]):
        cq_out[0, r, :] = _rms(lat[:, :rq], nq_ref[...]).astype(BF16)
        ckv_out[0, r, :] = _rms(lat[:, rq:], nkv_ref[...]).astype(BF16)
    project(wk_ref, k_out)


def _in_proj(x, p):
    B, L, D = x.shape
    tm = min(IN_TOKENS, L)
    grid = (B, L // tm)
    tok = lambda n, dt=BF16: (pl.BlockSpec((1, tm, n), lambda b, j: (b, j, 0)),
                              jax.ShapeDtypeStruct((B, L, n), dt))
    outs = [tok(p['wq'].shape[1]), tok(p['wk'].shape[1]), tok(p['wv'].shape[1]),
            tok(p['wg'].shape[1]), tok(p['wq'].shape[1], F32), tok(p['wq'].shape[1], F32),
            tok(p['nq'].shape[1]), tok(p['nkv'].shape[1]), tok(MLA_ROPE), tok(D), tok(D)]
    weights = [p['wq'], p['wk'], p['wv'], p['wg'], p['wlat'], p['wmisc'], p['wga'], p['wgb'],
               p['wa'], p['ba']]
    return pl.pallas_call(
        _in_proj_kernel,
        grid=grid,
        in_specs=[pl.BlockSpec((1, tm, D), lambda b, j: (b, j, 0)),
                  _resident(p['g_pre'].shape),
                  pl.BlockSpec((tm, 2 * MLA_ROPE), lambda b, j: (j, 0)),
                  _resident(p['nq'].shape), _resident(p['nkv'].shape)]
                 + [_resident(w.shape) for w in weights],
        out_specs=[o[0] for o in outs],
        out_shape=[o[1] for o in outs],
        compiler_params=_params(),
        name="in_proj",
    )(x, p['g_pre'], p['rope'], p['nq'], p['nkv'], *weights)


def _gla_kernel(q_ref, k_ref, v_ref, g_ref, laf_ref, lab_ref,
                gn_ref, o_ref, mf_ref, mb_ref, sf_ref, sb_ref, bf_ref, bb_ref,
                qif_ref, qib_ref, uf_ref, ub_ref, dtf_ref, dtb_ref, a0_ref, a1_ref, oacc_ref):
    L = q_ref.shape[1]
    C = GLA_CHUNK
    nc = L // C
    nr = min(NORM_ROWS, L)
    mid = C // 2 - 1
    scale = GLA_DK ** -0.5
    a_refs = (a0_ref, a1_ref)
    assert nc % 2 == 0

    row = lax.broadcasted_iota(jnp.int32, (C, C), 0)
    col = lax.broadcasted_iota(jnp.int32, (C, C), 1)
    mf_ref[...] = jnp.where(col <= row, 1.0, 0.0).astype(BF16)
    mb_ref[...] = jnp.where(col >= row, 1.0, 0.0).astype(BF16)

    def cumsum(c, carry):
        rows = pl.ds(pl.multiple_of(c * C, C), C)
        for m_ref, la_ref, b_ref in ((mf_ref, laf_ref, bf_ref), (mb_ref, lab_ref, bb_ref)):
            la = la_ref[0, rows, :]
            hi = la.astype(BF16)
            lo = (la - hi.astype(F32)).astype(BF16)
            cs = _dot(m_ref[...], jnp.concatenate([hi, lo], axis=1))
            b_ref[rows, :] = cs[:, :GLA_DK] + cs[:, GLA_DK:]
        return carry

    lax.fori_loop(0, nc, cumsum, 0, unroll=GLA_UNROLL)

    dirs = ((bf_ref, qif_ref, uf_ref, dtf_ref, C - 1, mid, True),
            (bb_ref, qib_ref, ub_ref, dtb_ref, 0, mid + 1, False))

    def chunk_rows(c):
        return pl.ds(c * C if isinstance(c, int) else pl.multiple_of(c * C, C), C)

    def intra_out(pair):
        for parity, a_ref in enumerate(a_refs):
            rows = chunk_rows(2 * pair + parity)
            oacc_ref[rows, :] = _dot(a_ref[chunk_rows(pair), :], v_ref[0, rows, :])

    for a_ref in a_refs:
        a_ref[:C, :] = jnp.zeros((C, C), BF16)

    def intra(pair, carry):
        intra_out(jnp.maximum(pair - 1, 0))
        for parity, a_ref in enumerate(a_refs):
            rows = chunk_rows(2 * pair + parity)
            q = q_ref[0, rows, :].astype(F32) * scale
            k = k_ref[0, rows, :].astype(F32)
            v = v_ref[0, rows, :]
            a_sum = None
            for b_ref, qi_ref, u_ref, dt_ref, tot_row, ref_row, fwd in dirs:
                b_incl = b_ref[rows, :]
                b_mid = b_incl[ref_row:ref_row + 1, :]
                b_tot = b_incl[tot_row:tot_row + 1, :]
                b_rel = b_incl - b_mid
                q_rel = q * jnp.exp2(b_rel)
                k_rel = k * jnp.exp2(-b_rel)
                qi_ref[rows, :] = (q_rel * jnp.exp2(b_mid)).astype(BF16)
                k_end = k_rel * jnp.exp2(b_tot - b_mid)
                mask = (row >= col) if fwd else (row <= col)
                a = jnp.where(mask, _dot_nt(q_rel.astype(BF16), k_rel.astype(BF16)), 0.0)
                a_sum = a if a_sum is None else a_sum + a
                u_ref[rows, :] = _dot(k_end.T.astype(BF16), v)
                dt_ref[rows, :] = jnp.broadcast_to(jnp.exp2(b_tot), (C, GLA_DK)).T
            a_ref[chunk_rows(pair), :] = a_sum.astype(BF16)
        return carry

    lax.fori_loop(0, nc // 2, intra, 0)
    intra_out(nc // 2 - 1)

    sf_ref[...] = jnp.zeros_like(sf_ref)
    sb_ref[...] = jnp.zeros_like(sb_ref)

    def scan(i, carry):
        for s_ref, qi_ref, u_ref, dt_ref, c in ((sf_ref, qif_ref, uf_ref, dtf_ref, i),
                                                (sb_ref, qib_ref, ub_ref, dtb_ref, nc - 1 - i)):
            rows = pl.ds(pl.multiple_of(c * C, C), C)
            s = s_ref[...]
            oacc_ref[rows, :] += _dot(qi_ref[rows, :], s.astype(BF16))
            dt = dt_ref[rows, :]
            s_ref[...] = s * jnp.concatenate([dt] * (GLA_DV // C), axis=1) + u_ref[rows, :]
        return carry

    lax.fori_loop(0, nc, scan, 0, unroll=GLA_UNROLL)

    for r in range(L // nr):
        rows = pl.ds(r * nr, nr)
        y = _rms(oacc_ref[rows, :], gn_ref[...])
        g = g_ref[0, rows, :].astype(F32)
        o_ref[0, rows, :] = (y * (g * jax.nn.sigmoid(g))).astype(BF16)


def _gla(q, k, v, g, la_f, la_b, p):
    B, L, _ = q.shape
    C = GLA_CHUNK
    head = lambda n: pl.BlockSpec((1, L, n), lambda b, h: (b, 0, h))
    return pl.pallas_call(
        _gla_kernel,
        grid=(B, GLA_HEADS),
        in_specs=[head(GLA_DK), head(GLA_DK), head(GLA_DV), head(GLA_DV),
                  head(GLA_DK), head(GLA_DK),
                  pl.BlockSpec((1, GLA_DV), lambda b, h: (0, 0))],
        out_specs=head(GLA_DV),
        out_shape=jax.ShapeDtypeStruct((B, L, GLA_HEADS * GLA_DV), BF16),
        scratch_shapes=[pltpu.VMEM((C, C), BF16), pltpu.VMEM((C, C), BF16),
                        pltpu.VMEM((GLA_DK, GLA_DV), F32), pltpu.VMEM((GLA_DK, GLA_DV), F32),
                        pltpu.VMEM((L, GLA_DK), F32), pltpu.VMEM((L, GLA_DK), F32),
                        pltpu.VMEM((L, GLA_DK), BF16), pltpu.VMEM((L, GLA_DK), BF16),
                        pltpu.VMEM((L, GLA_DV), F32), pltpu.VMEM((L, GLA_DV), F32),
                        pltpu.VMEM((L, GLA_DK), F32), pltpu.VMEM((L, GLA_DK), F32),
                        pltpu.VMEM((L // 2, C), BF16), pltpu.VMEM((L // 2, C), BF16),
                        pltpu.VMEM((L, GLA_DV), F32)],
        compiler_params=_params(),
        name="gla",
    )(q, k, v, g, la_f, la_b, p['gla_norm'])


def _mla_kernel(cq_ref, ckv_ref, kpe_ref, rope_ref, wq_ref, wkv_ref, o_ref,
                qs_ref, ks_ref, vs_ref):
    L = cq_ref.shape[1]
    scale = (MLA_NOPE + MLA_ROPE) ** -0.5 * LOG2_E
    qall = _dot(cq_ref[0], wq_ref[0])
    qs_ref[:, :MLA_NOPE] = (qall[:, :MLA_NOPE] * scale).astype(BF16)
    qs_ref[:, MLA_NOPE:] = (_rope(qall[:, MLA_NOPE:], rope_ref[...]) * scale).astype(BF16)
    kv = _dot(ckv_ref[0], wkv_ref[0])
    ks_ref[:, :MLA_NOPE] = kv[:, :MLA_NOPE].astype(BF16)
    ks_ref[:, MLA_NOPE:] = kpe_ref[0]
    vs_ref[...] = kv[:, MLA_NOPE:].astype(BF16)

    tq = min(Q_TOKENS, L)
    nq = L // tq
    scores = lambda i: _dot_nt(qs_ref[i * tq:(i + 1) * tq, :], ks_ref[...])
    s = scores(0)
    for i in range(nq):
        s_next = scores(i + 1) if i + 1 < nq else None
        e = jnp.exp2(s - jnp.max(s, axis=-1, keepdims=True))
        l = jnp.sum(e, axis=-1, keepdims=True)
        o = _dot(e.astype(BF16), vs_ref[...]) * (1.0 / l)
        o_ref[0, i * tq:(i + 1) * tq, :] = o.astype(BF16)
        s = s_next


def _mla(cq, ckv, kpe, p):
    B, L, R = cq.shape
    dqk = MLA_NOPE + MLA_ROPE
    seq = lambda n: pl.BlockSpec((1, L, n), lambda b, h: (b, 0, 0))
    per_head = lambda a: pl.BlockSpec((1,) + a.shape[1:], lambda b, h: (h, 0, 0))
    return pl.pallas_call(
        _mla_kernel,
        grid=(B, MLA_HEADS),
        in_specs=[seq(R), seq(ckv.shape[2]), seq(MLA_ROPE), _resident(p['rope'].shape),
                  per_head(p['wq_mla']), per_head(p['wkv_mla'])],
        out_specs=pl.BlockSpec((1, L, MLA_V), lambda b, h: (b, 0, h)),
        out_shape=jax.ShapeDtypeStruct((B, L, MLA_HEADS * MLA_V), BF16),
        scratch_shapes=[pltpu.VMEM((L, dqk), BF16), pltpu.VMEM((L, dqk), BF16),
                        pltpu.VMEM((L, MLA_V), BF16)],
        compiler_params=_params(),
        name="mla",
    )(cq, ckv, kpe, p['rope'], p['wq_mla'], p['wkv_mla'])


def _out_kernel(x_ref, oa_ref, ob_ref, ga_ref, gb_ref, woa_ref, wob_ref, wout_ref,
                wgate_ref, wup_ref, wdown_ref, npost_ref, nfpre_ref, nfpost_ref, y_ref):
    tm = x_ref.shape[1]
    parts = [pl.ds(i * (tm // OUT_PARTS), tm // OUT_PARTS) for i in range(OUT_PARTS)]
    ya = [_dot(oa_ref[0, r, :], woa_ref[...]) for r in parts]
    yb = [_dot(ob_ref[0, r, :], wob_ref[...]) for r in parts]
    merged = [(ga_ref[0, r, :].astype(F32) * a + gb_ref[0, r, :].astype(F32) * b).astype(BF16)
              for r, a, b in zip(parts, ya, yb)]
    mo = [_dot(m, wout_ref[...]) for m in merged]
    x1 = [x_ref[0, r, :] + _rms(o, npost_ref[...]) for r, o in zip(parts, mo)]
    h = [_rms(x, nfpre_ref[...]).astype(BF16) for x in x1]
    gate = [_dot(v, wgate_ref[...]) for v in h]
    up = [_dot(v, wup_ref[...]) for v in h]
    act = [(g * jax.nn.sigmoid(g) * u).astype(BF16) for g, u in zip(gate, up)]
    down = [_dot(a, wdown_ref[...]) for a in act]
    for r, x, d in zip(parts, x1, down):
        y_ref[0, r, :] = x + _rms(d, nfpost_ref[...])


def _out(x, oa, ob, ga, gb, p):
    B, L, D = x.shape
    tm = min(OUT_TOKENS, L)
    tok = pl.BlockSpec((1, tm, D), lambda b, j: (b, j, 0))
    consts = [p['w_o_gla'], p['w_o_mla'], p['w_out'], p['w_gate'], p['w_up'], p['w_down'],
              p['g_post'], p['g_ffn_pre'], p['g_ffn_post']]
    return pl.pallas_call(
        _out_kernel,
        grid=(B, L // tm),
        in_specs=[tok] * 5 + [_resident(c.shape) for c in consts],
        out_specs=tok,
        out_shape=jax.ShapeDtypeStruct((B, L, D), F32),
        compiler_params=_params(),
        name="merge_ffn",
    )(x, oa, ob, ga, gb, *consts)


def _prepare(L, norm_mix_pre, w_in, gla_wa_fwd, gla_ba_fwd, gla_wa_bwd, gla_ba_bwd, gla_norm,
             w_o_gla, mla_norm_q, w_uq, mla_norm_kv, w_uk, w_uv, w_o_mla, w_out,
             norm_mix_post, norm_ffn_pre, w_gate, w_up, w_down, norm_ffn_post):
    D = w_in.shape[0]
    qk = GLA_HEADS * GLA_DK
    vw = GLA_HEADS * GLA_DV
    rq, rkv = mla_norm_q.shape[0], mla_norm_kv.shape[0]
    widths = (qk, qk, vw, vw, GLA_RANK, GLA_RANK, rq, rkv, MLA_ROPE, D, D)
    offs = np.concatenate([[0], np.cumsum(widths)]).tolist()
    piece = lambda i, j=None: w_in[:, offs[i]:offs[(i if j is None else j) + 1]]
    swap = np.concatenate([np.arange(MLA_ROPE // 2, MLA_ROPE), np.arange(MLA_ROPE // 2)])
    w_kr = piece(8)
    row = lambda a: a.reshape(1, -1).astype(F32)

    inv = ROPE_BASE ** (-jnp.arange(0, MLA_ROPE, 2, dtype=F32) / MLA_ROPE)
    ang = jnp.arange(L, dtype=F32)[:, None] * inv[None, :]
    cos, sin = jnp.cos(ang), jnp.sin(ang)

    uq = w_uq.reshape(rq, MLA_HEADS, MLA_NOPE + MLA_ROPE)
    wq_pe = uq[:, :, MLA_NOPE:]
    wq_mla = jnp.concatenate([uq[:, :, :MLA_NOPE], wq_pe, wq_pe[:, :, swap]], axis=2)
    wkv_mla = jnp.concatenate([w_uk.reshape(rkv, MLA_HEADS, MLA_NOPE),
                               w_uv.reshape(rkv, MLA_HEADS, MLA_V)], axis=2)
    zeros = jnp.zeros((GLA_RANK, qk), F32)
    return dict(
        g_pre=row(norm_mix_pre),
        wq=piece(0).astype(BF16), wk=piece(1).astype(BF16), wv=piece(2).astype(BF16),
        wg=piece(3).astype(BF16), wlat=piece(6, 7).astype(BF16),
        wmisc=jnp.concatenate([w_kr, w_kr[:, swap], piece(4, 5)], axis=1).astype(BF16),
        wga=piece(9).astype(BF16), wgb=piece(10).astype(BF16),
        nq=row(mla_norm_q), nkv=row(mla_norm_kv),
        rope=jnp.concatenate([cos, cos, -sin, sin], axis=1),
        wa=jnp.concatenate([jnp.concatenate([gla_wa_fwd, zeros], axis=1),
                            jnp.concatenate([zeros, gla_wa_bwd], axis=1)], axis=0).astype(BF16),
        ba=jnp.concatenate([row(gla_ba_fwd), row(gla_ba_bwd)], axis=1), gla_norm=row(gla_norm),
        wq_mla=jnp.transpose(wq_mla, (1, 0, 2)).astype(BF16),
        wkv_mla=jnp.transpose(wkv_mla, (1, 0, 2)).astype(BF16),
        w_o_gla=w_o_gla.astype(BF16), w_o_mla=w_o_mla.astype(BF16), w_out=w_out.astype(BF16),
        w_gate=w_gate.astype(BF16), w_up=w_up.astype(BF16), w_down=w_down.astype(BF16),
        g_post=row(norm_mix_post), g_ffn_pre=row(norm_ffn_pre), g_ffn_post=row(norm_ffn_post),
    )


def _layer(x, p):
    q, k, v, g, la_f, la_b, cq, ckv, kpe, ga, gb = _in_proj(x, p)
    oa = _gla(q, k, v, g, la_f, la_b, p)
    ob = _mla(cq, ckv, kpe, p)
    return _out(x, oa, ob, ga, gb, p)


def kernel(x_prompt, x_sample, norm_mix_pre, w_in, gla_wa_fwd, gla_ba_fwd, gla_wa_bwd, gla_ba_bwd, gla_norm, w_o_gla, mla_norm_q, w_uq, mla_norm_kv, w_uk, w_uv, w_o_mla, w_out, norm_mix_post, norm_ffn_pre, w_gate, w_up, w_down, norm_ffn_post):
    assert x_prompt.shape[1] == x_sample.shape[1]
    stacked = (norm_mix_pre, w_in, gla_wa_fwd, gla_ba_fwd, gla_wa_bwd, gla_ba_bwd, gla_norm,
               w_o_gla, mla_norm_q, w_uq, mla_norm_kv, w_uk, w_uv, w_o_mla, w_out,
               norm_mix_post, norm_ffn_pre, w_gate, w_up, w_down, norm_ffn_post)
    y_prompt, y_sample = x_prompt, x_sample
    for layer in range(norm_mix_pre.shape[0]):
        p = _prepare(x_prompt.shape[1], *[a[layer] for a in stacked])
        y_prompt, y_sample = _layer(y_prompt, p), _layer(y_sample, p)
    return (y_prompt, y_sample)
```

```python
import jax
import jax.numpy as jnp
import numpy as np
from jax import lax
from jax.experimental import pallas as pl
from jax.experimental.pallas import tpu as pltpu

F32 = jnp.float32
BF16 = jnp.bfloat16

EPS = 1e-6
GLA_HEADS = 4
GLA_DK = 128
GLA_DV = 256
GLA_RANK = 16
GLA_TEMP = 16.0
GLA_CHUNK = 128
MLA_HEADS = 8
MLA_NOPE = 128
MLA_ROPE = 64
MLA_V = 128
ROPE_BASE = 10000.0

IN_TOKENS = 512
IN_PARTS = 2
OUT_TOKENS = 512
OUT_PARTS = 2
Q_TOKENS = 256
MXU_WIDTH_V7X = 256
LOG2_E = 1.4426950408889634
GLA_UNROLL = 4
NORM_ROWS = 256
VMEM_LIMIT_BYTES = 56 * 1024 * 1024


def _rms(xf, g):
    ms = jnp.mean(xf * xf, axis=-1, keepdims=True)
    return xf * lax.rsqrt(ms + EPS) * g


def _silu(x):
    h = 0.5 * x
    return h + h * jnp.tanh(h)


def _dot(a, b):
    return jnp.dot(a, b, preferred_element_type=F32)


def _dot_nt(a, b):
    return lax.dot_general(a, b, (((1,), (1,)), ((), ())), preferred_element_type=F32)


def _rope(x2, table):
    y = x2 * table
    return (y + pltpu.roll(y, MLA_ROPE, axis=1))[:, :MLA_ROPE]


def _params():
    return pltpu.CompilerParams(
        dimension_semantics=("arbitrary", "arbitrary"),
        vmem_limit_bytes=VMEM_LIMIT_BYTES,
    )


def _resident(shape):
    nd = len(shape)
    return pl.BlockSpec(shape, lambda *_: (0,) * nd, pipeline_mode=pl.Buffered(1))


def _in_proj_kernel(x_ref, g_ref, rope_ref, nq_ref, nkv_ref,
                    wq_ref, wk_ref, wv_ref, wg_ref, wlat_ref, wmisc_ref, wga_ref, wgb_ref,
                    wa_ref, ba_ref,
                    q_out, k_out, v_out, g_out, laf_out, lab_out, cq_out, ckv_out, kpe_out,
                    ga_out, gb_out):
    tm = x_ref.shape[1]
    parts = [pl.ds(i * (tm // IN_PARTS), tm // IN_PARTS) for i in range(IN_PARTS)]
    h = [_rms(x_ref[0, r, :], g_ref[...]).astype(BF16) for r in parts]

    def project(w_ref, out, epilogue=lambda y: y):
        for r, y in zip(parts, [_dot(v, w_ref[...]) for v in h]):
            out[0, r, :] = epilogue(y).astype(out.dtype)

    misc = [_dot(v, wmisc_ref[...]) for v in h]
    for r, m in zip(parts, misc):
        kpe_out[0, r, :] = _rope(m[:, :2 * MLA_ROPE], rope_ref[r, :]).astype(BF16)

    lat_cols = slice(2 * MLA_ROPE, 2 * MLA_ROPE + 2 * GLA_RANK)
    zs = [_dot(m[:, lat_cols].astype(BF16), wa_ref[...]) + ba_ref[...] for m in misc]
    qk = GLA_HEADS * GLA_DK
    for r, z in zip(parts, zs):
        la = (jnp.minimum(z, 0.0) - jnp.log(1.0 + jnp.exp(-jnp.abs(z)))) * (LOG2_E / GLA_TEMP)
        laf_out[0, r, :] = la[:, :qk]
        lab_out[0, r, :] = la[:, qk:]

    project(wv_ref, v_out)
    project(wg_ref, g_out)
    project(wga_ref, ga_out, jax.nn.sigmoid)
    project(wq_ref, q_out)
    project(wgb_ref, gb_out, jax.nn.sigmoid)
    rq = nq_ref.shape[1]
    for r, lat in zip(parts, [_dot(v, wlat_ref[...]) for v in h]):
        cq_out[0, r, :] = _rms(lat[:, :rq], nq_ref[...]).astype(BF16)
        ckv_out[0, r, :] = _rms(lat[:, rq:], nkv_ref[...]).astype(BF16)
    project(wk_ref, k_out)


def _in_proj(x, p):
    B, L, D = x.shape
    tm = min(IN_TOKENS, L)
    grid = (B, L // tm)
    tok = lambda n, dt=BF16: (pl.BlockSpec((1, tm, n), lambda b, j: (b, j, 0)),
                              jax.ShapeDtypeStruct((B, L, n), dt))
    outs = [tok(p['wq'].shape[1]), tok(p['wk'].shape[1]), tok(p['wv'].shape[1]),
            tok(p['wg'].shape[1]), tok(p['wq'].shape[1], F32), tok(p['wq'].shape[1], F32),
            tok(p['nq'].shape[1]), tok(p['nkv'].shape[1]), tok(MLA_ROPE), tok(D), tok(D)]
    weights = [p['wq'], p['wk'], p['wv'], p['wg'], p['wlat'], p['wmisc'], p['wga'], p['wgb'],
               p['wa'], p['ba']]
    return pl.pallas_call(
        _in_proj_kernel,
        grid=grid,
        in_specs=[pl.BlockSpec((1, tm, D), lambda b, j: (b, j, 0)),
                  _resident(p['g_pre'].shape),
                  pl.BlockSpec((tm, 2 * MLA_ROPE), lambda b, j: (j, 0)),
                  _resident(p['nq'].shape), _resident(p['nkv'].shape)]
                 + [_resident(w.shape) for w in weights],
        out_specs=[o[0] for o in outs],
        out_shape=[o[1] for o in outs],
        compiler_params=_params(),
        name="in_proj",
    )(x, p['g_pre'], p['rope'], p['nq'], p['nkv'], *weights)


def _gla_kernel(q_ref, k_ref, v_ref, g_ref, laf_ref, lab_ref,
                gn_ref, o_ref, mf_ref, mb_ref, sf_ref, sb_ref, bf_ref, bb_ref,
                qif_ref, qib_ref, uf_ref, ub_ref, dtf_ref, dtb_ref, a0_ref, a1_ref, oacc_ref):
    L = q_ref.shape[1]
    C = GLA_CHUNK
    nc = L // C
    nr = min(NORM_ROWS, L)
    mid = C // 2 - 1
    scale = GLA_DK ** -0.5
    a_refs = (a0_ref, a1_ref)
    assert nc % 2 == 0

    row = lax.broadcasted_iota(jnp.int32, (C, C), 0)
    col = lax.broadcasted_iota(jnp.int32, (C, C), 1)
    mf_ref[...] = jnp.where(col <= row, 1.0, 0.0).astype(BF16)
    mb_ref[...] = jnp.where(col >= row, 1.0, 0.0).astype(BF16)

    def cumsum(c, carry):
        rows = pl.ds(pl.multiple_of(c * C, C), C)
        for m_ref, la_ref, b_ref in ((mf_ref, laf_ref, bf_ref), (mb_ref, lab_ref, bb_ref)):
            la = la_ref[0, rows, :]
            hi = la.astype(BF16)
            lo = (la - hi.astype(F32)).astype(BF16)
            cs = _dot(m_ref[...], jnp.concatenate([hi, lo], axis=1))
            b_ref[rows, :] = cs[:, :GLA_DK] + cs[:, GLA_DK:]
        return carry

    lax.fori_loop(0, nc, cumsum, 0, unroll=GLA_UNROLL)

    dirs = ((bf_ref, qif_ref, uf_ref, dtf_ref, C - 1, mid, True),
            (bb_ref, qib_ref, ub_ref, dtb_ref, 0, mid + 1, False))

    def chunk_rows(c):
        return pl.ds(c * C if isinstance(c, int) else pl.multiple_of(c * C, C), C)

    def intra_out(pair):
        for parity, a_ref in enumerate(a_refs):
            rows = chunk_rows(2 * pair + parity)
            oacc_ref[rows, :] = _dot(a_ref[chunk_rows(pair), :], v_ref[0, rows, :])

    for a_ref in a_refs:
        a_ref[:C, :] = jnp.zeros((C, C), BF16)

    def intra(pair, carry):
        intra_out(jnp.maximum(pair - 1, 0))
        for parity, a_ref in enumerate(a_refs):
            rows = chunk_rows(2 * pair + parity)
            q = q_ref[0, rows, :].astype(F32) * scale
            k = k_ref[0, rows, :].astype(F32)
            v = v_ref[0, rows, :]
            a_sum = None
            for b_ref, qi_ref, u_ref, dt_ref, tot_row, ref_row, fwd in dirs:
                b_incl = b_ref[rows, :]
                b_mid = b_incl[ref_row:ref_row + 1, :]
                b_tot = b_incl[tot_row:tot_row + 1, :]
                b_rel = b_incl - b_mid
                q_rel = q * jnp.exp2(b_rel)
                k_rel = k * jnp.exp2(-b_rel)
                qi_ref[rows, :] = (q_rel * jnp.exp2(b_mid)).astype(BF16)
                k_end = k_rel * jnp.exp2(b_tot - b_mid)
                mask = (row >= col) if fwd else (row <= col)
                a = jnp.where(mask, _dot_nt(q_rel.astype(BF16), k_rel.astype(BF16)), 0.0)
                a_sum = a if a_sum is None else a_sum + a
                u_ref[rows, :] = _dot(k_end.T.astype(BF16), v)
                dt_ref[rows, :] = jnp.broadcast_to(jnp.exp2(b_tot), (C, GLA_DK)).T
            a_ref[chunk_rows(pair), :] = a_sum.astype(BF16)
        return carry

    lax.fori_loop(0, nc // 2, intra, 0)
    intra_out(nc // 2 - 1)

    sf_ref[...] = jnp.zeros_like(sf_ref)
    sb_ref[...] = jnp.zeros_like(sb_ref)

    def scan(i, carry):
        for s_ref, qi_ref, u_ref, dt_ref, c in ((sf_ref, qif_ref, uf_ref, dtf_ref, i),
                                                (sb_ref, qib_ref, ub_ref, dtb_ref, nc - 1 - i)):
            rows = pl.ds(pl.multiple_of(c * C, C), C)
            s = s_ref[...]
            oacc_ref[rows, :] += _dot(qi_ref[rows, :], s.astype(BF16))
            dt = dt_ref[rows, :]
            s_ref[...] = s * jnp.concatenate([dt] * (GLA_DV // C), axis=1) + u_ref[rows, :]
        return carry

    lax.fori_loop(0, nc, scan, 0, unroll=GLA_UNROLL)

    for r in range(L // nr):
        rows = pl.ds(r * nr, nr)
        y = _rms(oacc_ref[rows, :], gn_ref[...])
        g = g_ref[0, rows, :].astype(F32)
        o_ref[0, rows, :] = (y * _silu(g)).astype(BF16)


def _gla(q, k, v, g, la_f, la_b, p):
    B, L, _ = q.shape
    C = GLA_CHUNK
    head = lambda n: pl.BlockSpec((1, L, n), lambda b, h: (b, 0, h))
    return pl.pallas_call(
        _gla_kernel,
        grid=(B, GLA_HEADS),
        in_specs=[head(GLA_DK), head(GLA_DK), head(GLA_DV), head(GLA_DV),
                  head(GLA_DK), head(GLA_DK),
                  pl.BlockSpec((1, GLA_DV), lambda b, h: (0, 0))],
        out_specs=head(GLA_DV),
        out_shape=jax.ShapeDtypeStruct((B, L, GLA_HEADS * GLA_DV), BF16),
        scratch_shapes=[pltpu.VMEM((C, C), BF16), pltpu.VMEM((C, C), BF16),
                        pltpu.VMEM((GLA_DK, GLA_DV), F32), pltpu.VMEM((GLA_DK, GLA_DV), F32),
                        pltpu.VMEM((L, GLA_DK), F32), pltpu.VMEM((L, GLA_DK), F32),
                        pltpu.VMEM((L, GLA_DK), BF16), pltpu.VMEM((L, GLA_DK), BF16),
                        pltpu.VMEM((L, GLA_DV), F32), pltpu.VMEM((L, GLA_DV), F32),
                        pltpu.VMEM((L, GLA_DK), F32), pltpu.VMEM((L, GLA_DK), F32),
                        pltpu.VMEM((L // 2, C), BF16), pltpu.VMEM((L // 2, C), BF16),
                        pltpu.VMEM((L, GLA_DV), F32)],
        compiler_params=_params(),
        name="gla",
    )(q, k, v, g, la_f, la_b, p['gla_norm'])


def _mla_kernel(cq_ref, ckv_ref, kpe_ref, rope_ref, wq_ref, wkv_ref, o_ref,
                qs_ref, ks_ref, vs_ref):
    L = cq_ref.shape[1]
    scale = (MLA_NOPE + MLA_ROPE) ** -0.5 * LOG2_E
    qall = _dot(cq_ref[0], wq_ref[0])
    qs_ref[:, :MLA_NOPE] = (qall[:, :MLA_NOPE] * scale).astype(BF16)
    qs_ref[:, MLA_NOPE:] = (_rope(qall[:, MLA_NOPE:], rope_ref[...]) * scale).astype(BF16)
    kv = _dot(ckv_ref[0], wkv_ref[0])
    ks_ref[:, :MLA_NOPE] = kv[:, :MLA_NOPE].astype(BF16)
    ks_ref[:, MLA_NOPE:] = kpe_ref[0]
    vs_ref[:, :MLA_V] = kv[:, MLA_NOPE:].astype(BF16)
    vs_ref[:, MLA_V:] = jnp.ones((L, MXU_WIDTH_V7X - MLA_V), BF16)

    tq = min(Q_TOKENS, L)
    nq = L // tq
    scores = lambda i: _dot_nt(qs_ref[i * tq:(i + 1) * tq, :], ks_ref[...])
    s = scores(0)
    for i in range(nq):
        s_next = scores(i + 1) if i + 1 < nq else None
        e = jnp.exp2(s - jnp.max(s, axis=-1, keepdims=True)).astype(BF16)
        o_ext = _dot(e, vs_ref[...])
        o = o_ext[:, :MLA_V] * (1.0 / o_ext[:, MLA_V:MLA_V + 1])
        o_ref[0, i * tq:(i + 1) * tq, :] = o.astype(BF16)
        s = s_next


def _mla(cq, ckv, kpe, p):
    B, L, R = cq.shape
    dqk = MLA_NOPE + MLA_ROPE
    seq = lambda n: pl.BlockSpec((1, L, n), lambda b, h: (b, 0, 0))
    per_head = lambda a: pl.BlockSpec((1,) + a.shape[1:], lambda b, h: (h, 0, 0))
    return pl.pallas_call(
        _mla_kernel,
        grid=(B, MLA_HEADS),
        in_specs=[seq(R), seq(ckv.shape[2]), seq(MLA_ROPE), _resident(p['rope'].shape),
                  per_head(p['wq_mla']), per_head(p['wkv_mla'])],
        out_specs=pl.BlockSpec((1, L, MLA_V), lambda b, h: (b, 0, h)),
        out_shape=jax.ShapeDtypeStruct((B, L, MLA_HEADS * MLA_V), BF16),
        scratch_shapes=[pltpu.VMEM((L, dqk), BF16), pltpu.VMEM((L, dqk), BF16),
                        pltpu.VMEM((L, MXU_WIDTH_V7X), BF16)],
        compiler_params=_params(),
        name="mla",
    )(cq, ckv, kpe, p['rope'], p['wq_mla'], p['wkv_mla'])


def _out_kernel(x_ref, oa_ref, ob_ref, ga_ref, gb_ref, woa_ref, wob_ref, wout_ref,
                wgate_ref, wup_ref, wdown_ref, npost_ref, nfpre_ref, nfpost_ref, y_ref):
    tm = x_ref.shape[1]
    parts = [pl.ds(i * (tm // OUT_PARTS), tm // OUT_PARTS) for i in range(OUT_PARTS)]
    ya = [_dot(oa_ref[0, r, :], woa_ref[...]) for r in parts]
    yb = [_dot(ob_ref[0, r, :], wob_ref[...]) for r in parts]
    merged = [(ga_ref[0, r, :].astype(F32) * a + gb_ref[0, r, :].astype(F32) * b).astype(BF16)
              for r, a, b in zip(parts, ya, yb)]
    mo = [_dot(m, wout_ref[...]) for m in merged]
    x1 = [x_ref[0, r, :] + _rms(o, npost_ref[...]) for r, o in zip(parts, mo)]
    h = [_rms(x, nfpre_ref[...]).astype(BF16) for x in x1]
    gate = [_dot(v, wgate_ref[...]) for v in h]
    up = [_dot(v, wup_ref[...]) for v in h]
    act = [(_silu(g) * u).astype(BF16) for g, u in zip(gate, up)]
    down = [_dot(a, wdown_ref[...]) for a in act]
    for r, x, d in zip(parts, x1, down):
        y_ref[0, r, :] = x + _rms(d, nfpost_ref[...])


def _out(x, oa, ob, ga, gb, p):
    B, L, D = x.shape
    tm = min(OUT_TOKENS, L)
    tok = pl.BlockSpec((1, tm, D), lambda b, j: (b, j, 0))
    consts = [p['w_o_gla'], p['w_o_mla'], p['w_out'], p['w_gate'], p['w_up'], p['w_down'],
              p['g_post'], p['g_ffn_pre'], p['g_ffn_post']]
    return pl.pallas_call(
        _out_kernel,
        grid=(B, L // tm),
        in_specs=[tok] * 5 + [_resident(c.shape) for c in consts],
        out_specs=tok,
        out_shape=jax.ShapeDtypeStruct((B, L, D), F32),
        compiler_params=_params(),
        name="merge_ffn",
    )(x, oa, ob, ga, gb, *consts)


def _prepare(L, norm_mix_pre, w_in, gla_wa_fwd, gla_ba_fwd, gla_wa_bwd, gla_ba_bwd, gla_norm,
             w_o_gla, mla_norm_q, w_uq, mla_norm_kv, w_uk, w_uv, w_o_mla, w_out,
             norm_mix_post, norm_ffn_pre, w_gate, w_up, w_down, norm_ffn_post):
    D = w_in.shape[0]
    qk = GLA_HEADS * GLA_DK
    vw = GLA_HEADS * GLA_DV
    rq, rkv = mla_norm_q.shape[0], mla_norm_kv.shape[0]
    widths = (qk, qk, vw, vw, GLA_RANK, GLA_RANK, rq, rkv, MLA_ROPE, D, D)
    offs = np.concatenate([[0], np.cumsum(widths)]).tolist()
    piece = lambda i, j=None: w_in[:, offs[i]:offs[(i if j is None else j) + 1]]
    swap = np.concatenate([np.arange(MLA_ROPE // 2, MLA_ROPE), np.arange(MLA_ROPE // 2)])
    w_kr = piece(8)
    row = lambda a: a.reshape(1, -1).astype(F32)

    inv = ROPE_BASE ** (-jnp.arange(0, MLA_ROPE, 2, dtype=F32) / MLA_ROPE)
    ang = jnp.arange(L, dtype=F32)[:, None] * inv[None, :]
    cos, sin = jnp.cos(ang), jnp.sin(ang)

    uq = w_uq.reshape(rq, MLA_HEADS, MLA_NOPE + MLA_ROPE)
    wq_pe = uq[:, :, MLA_NOPE:]
    wq_mla = jnp.concatenate([uq[:, :, :MLA_NOPE], wq_pe, wq_pe[:, :, swap]], axis=2)
    wkv_mla = jnp.concatenate([w_uk.reshape(rkv, MLA_HEADS, MLA_NOPE),
                               w_uv.reshape(rkv, MLA_HEADS, MLA_V)], axis=2)
    zeros = jnp.zeros((GLA_RANK, qk), F32)
    return dict(
        g_pre=row(norm_mix_pre),
        wq=piece(0).astype(BF16), wk=piece(1).astype(BF16), wv=piece(2).astype(BF16),
        wg=piece(3).astype(BF16), wlat=piece(6, 7).astype(BF16),
        wmisc=jnp.concatenate([w_kr, w_kr[:, swap], piece(4, 5)], axis=1).astype(BF16),
        wga=piece(9).astype(BF16), wgb=piece(10).astype(BF16),
        nq=row(mla_norm_q), nkv=row(mla_norm_kv),
        rope=jnp.concatenate([cos, cos, -sin, sin], axis=1),
        wa=jnp.concatenate([jnp.concatenate([gla_wa_fwd, zeros], axis=1),
                            jnp.concatenate([zeros, gla_wa_bwd], axis=1)], axis=0).astype(BF16),
        ba=jnp.concatenate([row(gla_ba_fwd), row(gla_ba_bwd)], axis=1), gla_norm=row(gla_norm),
        wq_mla=jnp.transpose(wq_mla, (1, 0, 2)).astype(BF16),
        wkv_mla=jnp.transpose(wkv_mla, (1, 0, 2)).astype(BF16),
        w_o_gla=w_o_gla.astype(BF16), w_o_mla=w_o_mla.astype(BF16), w_out=w_out.astype(BF16),
        w_gate=w_gate.astype(BF16), w_up=w_up.astype(BF16), w_down=w_down.astype(BF16),
        g_post=row(norm_mix_post), g_ffn_pre=row(norm_ffn_pre), g_ffn_post=row(norm_ffn_post),
    )


def _layer(x, p):
    q, k, v, g, la_f, la_b, cq, ckv, kpe, ga, gb = _in_proj(x, p)
    oa = _gla(q, k, v, g, la_f, la_b, p)
    ob = _mla(cq, ckv, kpe, p)
    return _out(x, oa, ob, ga, gb, p)


def kernel(x_prompt, x_sample, norm_mix_pre, w_in, gla_wa_fwd, gla_ba_fwd, gla_wa_bwd, gla_ba_bwd, gla_norm, w_o_gla, mla_norm_q, w_uq, mla_norm_kv, w_uk, w_uv, w_o_mla, w_out, norm_mix_post, norm_ffn_pre, w_gate, w_up, w_down, norm_ffn_post):
    assert x_prompt.shape[1] == x_sample.shape[1]
    stacked = (norm_mix_pre, w_in, gla_wa_fwd, gla_ba_fwd, gla_wa_bwd, gla_ba_bwd, gla_norm,
               w_o_gla, mla_norm_q, w_uq, mla_norm_kv, w_uk, w_uv, w_o_mla, w_out,
               norm_mix_post, norm_ffn_pre, w_gate, w_up, w_down, norm_ffn_post)
    y_prompt, y_sample = x_prompt, x_sample
    for layer in range(norm_mix_pre.shape[0]):
        p = _prepare(x_prompt.shape[1], *[a[layer] for a in stacked])
        y_prompt, y_sample = _layer(y_prompt, p), _layer(y_sample, p)
    return (y_prompt, y_sample)
```

```python
import jax
import jax.numpy as jnp
import numpy as np
from jax import lax
from jax.experimental import pallas as pl
from jax.experimental.pallas import tpu as pltpu

F32 = jnp.float32
BF16 = jnp.bfloat16

EPS = 1e-6
GLA_HEADS = 4
GLA_DK = 128
GLA_DV = 256
GLA_RANK = 16
GLA_TEMP = 16.0
GLA_CHUNK = 128
MLA_HEADS = 8
MLA_NOPE = 128
MLA_ROPE = 64
MLA_V = 128
ROPE_BASE = 10000.0

IN_TOKENS = 512
IN_PARTS = 2
OUT_TOKENS = 512
OUT_PARTS = 2
Q_TOKENS = 256
Q_AHEAD = 1
MXU_WIDTH_V7X = 256
LOG2_E = 1.4426950408889634
GLA_UNROLL = 16
INTRA_GROUP = 4
NORM_ROWS = 256
VMEM_LIMIT_BYTES = 56 * 1024 * 1024


def _rms(xf, g):
    ms = jnp.mean(xf * xf, axis=-1, keepdims=True)
    return xf * lax.rsqrt(ms + EPS) * g


def _silu(x):
    h = 0.5 * x
    return h + h * jnp.tanh(h)


def _dot(a, b):
    return jnp.dot(a, b, preferred_element_type=F32)


def _dot_nt(a, b):
    return lax.dot_general(a, b, (((1,), (1,)), ((), ())), preferred_element_type=F32)


def _rope(x2, table):
    y = x2 * table
    return (y + pltpu.roll(y, MLA_ROPE, axis=1))[:, :MLA_ROPE]


def _params():
    return pltpu.CompilerParams(
        dimension_semantics=("arbitrary", "arbitrary"),
        vmem_limit_bytes=VMEM_LIMIT_BYTES,
    )


def _resident(shape):
    nd = len(shape)
    return pl.BlockSpec(shape, lambda *_: (0,) * nd, pipeline_mode=pl.Buffered(1))


def _in_proj_kernel(x_ref, g_ref, rope_ref, nq_ref, nkv_ref,
                    wq_ref, wk_ref, wv_ref, wg_ref, wlat_ref, wmisc_ref, wga_ref, wgb_ref,
                    wa_ref, ba_ref,
                    q_out, k_out, v_out, g_out, laf_out, lab_out, cq_out, ckv_out, kpe_out,
                    ga_out, gb_out):
    tm = x_ref.shape[1]
    parts = [pl.ds(i * (tm // IN_PARTS), tm // IN_PARTS) for i in range(IN_PARTS)]
    h = [_rms(x_ref[0, r, :], g_ref[...]).astype(BF16) for r in parts]

    def project(w_ref, out, epilogue=lambda y: y):
        for r, y in zip(parts, [_dot(v, w_ref[...]) for v in h]):
            out[0, r, :] = epilogue(y).astype(out.dtype)

    misc = [_dot(v, wmisc_ref[...]) for v in h]
    for r, m in zip(parts, misc):
        kpe_out[0, r, :] = _rope(m[:, :2 * MLA_ROPE], rope_ref[r, :]).astype(BF16)

    lat_cols = slice(2 * MLA_ROPE, 2 * MLA_ROPE + 2 * GLA_RANK)
    zs = [_dot(m[:, lat_cols].astype(BF16), wa_ref[...]) + ba_ref[...] for m in misc]
    qk = GLA_HEADS * GLA_DK
    for r, z in zip(parts, zs):
        la = (jnp.minimum(z, 0.0) - jnp.log(1.0 + jnp.exp(-jnp.abs(z)))) * (LOG2_E / GLA_TEMP)
        laf_out[0, r, :] = la[:, :qk]
        lab_out[0, r, :] = la[:, qk:]

    project(wv_ref, v_out)
    project(wg_ref, g_out)
    project(wga_ref, ga_out, jax.nn.sigmoid)
    project(wq_ref, q_out)
    project(wgb_ref, gb_out, jax.nn.sigmoid)
    rq = nq_ref.shape[1]
    for r, lat in zip(parts, [_dot(v, wlat_ref[...]) for v in h]):
        cq_out[0, r, :] = _rms(lat[:, :rq], nq_ref[...]).astype(BF16)
        ckv_out[0, r, :] = _rms(lat[:, rq:], nkv_ref[...]).astype(BF16)
    project(wk_ref, k_out)


def _in_proj(x, p):
    B, L, D = x.shape
    tm = min(IN_TOKENS, L)
    grid = (B, L // tm)
    tok = lambda n, dt=BF16: (pl.BlockSpec((1, tm, n), lambda b, j: (b, j, 0)),
                              jax.ShapeDtypeStruct((B, L, n), dt))
    outs = [tok(p['wq'].shape[1]), tok(p['wk'].shape[1]), tok(p['wv'].shape[1]),
            tok(p['wg'].shape[1]), tok(p['wq'].shape[1], F32), tok(p['wq'].shape[1], F32),
            tok(p['nq'].shape[1]), tok(p['nkv'].shape[1]), tok(MLA_ROPE), tok(D), tok(D)]
    weights = [p['wq'], p['wk'], p['wv'], p['wg'], p['wlat'], p['wmisc'], p['wga'], p['wgb'],
               p['wa'], p['ba']]
    return pl.pallas_call(
        _in_proj_kernel,
        grid=grid,
        in_specs=[pl.BlockSpec((1, tm, D), lambda b, j: (b, j, 0)),
                  _resident(p['g_pre'].shape),
                  pl.BlockSpec((tm, 2 * MLA_ROPE), lambda b, j: (j, 0)),
                  _resident(p['nq'].shape), _resident(p['nkv'].shape)]
                 + [_resident(w.shape) for w in weights],
        out_specs=[o[0] for o in outs],
        out_shape=[o[1] for o in outs],
        compiler_params=_params(),
        name="in_proj",
    )(x, p['g_pre'], p['rope'], p['nq'], p['nkv'], *weights)


def _gla_kernel(q_ref, k_ref, v_ref, g_ref, laf_ref, lab_ref,
                gn_ref, o_ref, mf_ref, mb_ref, sf_ref, sb_ref, bf_ref, bb_ref,
                qif_ref, qib_ref, uf_ref, ub_ref, dtf_ref, dtb_ref, *rest):
    L = q_ref.shape[1]
    C = GLA_CHUNK
    nc = L // C
    nr = min(NORM_ROWS, L)
    mid = C // 2 - 1
    scale = GLA_DK ** -0.5
    a_refs, oacc_ref = rest[:-1], rest[-1]
    group = len(a_refs)
    assert nc % group == 0

    row = lax.broadcasted_iota(jnp.int32, (C, C), 0)
    col = lax.broadcasted_iota(jnp.int32, (C, C), 1)
    mf_ref[...] = jnp.where(col <= row, 1.0, 0.0).astype(BF16)
    mb_ref[...] = jnp.where(col >= row, 1.0, 0.0).astype(BF16)

    def cumsum(c, carry):
        rows = pl.ds(pl.multiple_of(c * C, C), C)
        for m_ref, la_ref, b_ref in ((mf_ref, laf_ref, bf_ref), (mb_ref, lab_ref, bb_ref)):
            la = la_ref[0, rows, :]
            hi = la.astype(BF16)
            lo = (la - hi.astype(F32)).astype(BF16)
            cs = _dot(m_ref[...], jnp.concatenate([hi, lo], axis=1))
            b_ref[rows, :] = cs[:, :GLA_DK] + cs[:, GLA_DK:]
        return carry

    lax.fori_loop(0, nc, cumsum, 0, unroll=GLA_UNROLL)

    dirs = ((bf_ref, qif_ref, uf_ref, dtf_ref, C - 1, mid, True),
            (bb_ref, qib_ref, ub_ref, dtb_ref, 0, mid + 1, False))

    def chunk_rows(c):
        return pl.ds(c * C if isinstance(c, int) else pl.multiple_of(c * C, C), C)

    def intra_out(grp):
        for slot, a_ref in enumerate(a_refs):
            rows = chunk_rows(group * grp + slot)
            oacc_ref[rows, :] = _dot(a_ref[chunk_rows(grp), :], v_ref[0, rows, :])

    for a_ref in a_refs:
        a_ref[:C, :] = jnp.zeros((C, C), BF16)

    def intra(grp, carry):
        intra_out(jnp.maximum(grp - 1, 0))
        for slot, a_ref in enumerate(a_refs):
            rows = chunk_rows(group * grp + slot)
            q = q_ref[0, rows, :].astype(F32) * scale
            k = k_ref[0, rows, :].astype(F32)
            v = v_ref[0, rows, :]
            a_sum = None
            for b_ref, qi_ref, u_ref, dt_ref, tot_row, ref_row, fwd in dirs:
                b_incl = b_ref[rows, :]
                b_mid = b_incl[ref_row:ref_row + 1, :]
                b_tot = b_incl[tot_row:tot_row + 1, :]
                b_rel = b_incl - b_mid
                q_rel = q * jnp.exp2(b_rel)
                k_rel = k * jnp.exp2(-b_rel)
                qi_ref[rows, :] = (q_rel * jnp.exp2(b_mid)).astype(BF16)
                k_end = k_rel * jnp.exp2(b_tot - b_mid)
                mask = (row >= col) if fwd else (row <= col)
                a = jnp.where(mask, _dot_nt(q_rel.astype(BF16), k_rel.astype(BF16)), 0.0)
                a_sum = a if a_sum is None else a_sum + a
                u_ref[rows, :] = _dot(k_end.T.astype(BF16), v)
                dt_ref[rows, :] = jnp.broadcast_to(jnp.exp2(b_tot), (C, GLA_DK)).T
            a_ref[chunk_rows(grp), :] = a_sum.astype(BF16)
        return carry

    lax.fori_loop(0, nc // group, intra, 0)
    intra_out(nc // group - 1)

    sf_ref[...] = jnp.zeros_like(sf_ref)
    sb_ref[...] = jnp.zeros_like(sb_ref)

    def scan(i, carry):
        for s_ref, qi_ref, u_ref, dt_ref, c in ((sf_ref, qif_ref, uf_ref, dtf_ref, i),
                                                (sb_ref, qib_ref, ub_ref, dtb_ref, nc - 1 - i)):
            rows = pl.ds(pl.multiple_of(c * C, C), C)
            s = s_ref[...]
            oacc_ref[rows, :] += _dot(qi_ref[rows, :], s.astype(BF16))
            dt = dt_ref[rows, :]
            s_ref[...] = s * jnp.concatenate([dt] * (GLA_DV // C), axis=1) + u_ref[rows, :]
        return carry

    lax.fori_loop(0, nc, scan, 0, unroll=GLA_UNROLL)

    for r in range(L // nr):
        rows = pl.ds(r * nr, nr)
        y = _rms(oacc_ref[rows, :], gn_ref[...])
        g = g_ref[0, rows, :].astype(F32)
        o_ref[0, rows, :] = (y * _silu(g)).astype(BF16)


def _gla(q, k, v, g, la_f, la_b, p):
    B, L, _ = q.shape
    C = GLA_CHUNK
    head = lambda n: pl.BlockSpec((1, L, n), lambda b, h: (b, 0, h))
    return pl.pallas_call(
        _gla_kernel,
        grid=(B, GLA_HEADS),
        in_specs=[head(GLA_DK), head(GLA_DK), head(GLA_DV), head(GLA_DV),
                  head(GLA_DK), head(GLA_DK),
                  pl.BlockSpec((1, GLA_DV), lambda b, h: (0, 0))],
        out_specs=head(GLA_DV),
        out_shape=jax.ShapeDtypeStruct((B, L, GLA_HEADS * GLA_DV), BF16),
        scratch_shapes=[pltpu.VMEM((C, C), BF16), pltpu.VMEM((C, C), BF16),
                        pltpu.VMEM((GLA_DK, GLA_DV), F32), pltpu.VMEM((GLA_DK, GLA_DV), F32),
                        pltpu.VMEM((L, GLA_DK), F32), pltpu.VMEM((L, GLA_DK), F32),
                        pltpu.VMEM((L, GLA_DK), BF16), pltpu.VMEM((L, GLA_DK), BF16),
                        pltpu.VMEM((L, GLA_DV), F32), pltpu.VMEM((L, GLA_DV), F32),
                        pltpu.VMEM((L, GLA_DK), F32), pltpu.VMEM((L, GLA_DK), F32),
                        *[pltpu.VMEM((L // INTRA_GROUP, C), BF16) for _ in range(INTRA_GROUP)],
                        pltpu.VMEM((L, GLA_DV), F32)],
        compiler_params=_params(),
        name="gla",
    )(q, k, v, g, la_f, la_b, p['gla_norm'])


def _mla_kernel(cq_ref, ckv_ref, kpe_ref, rope_ref, wq_ref, wkv_ref, o_ref,
                qs_ref, ks_ref, vs_ref):
    L = cq_ref.shape[1]
    scale = (MLA_NOPE + MLA_ROPE) ** -0.5 * LOG2_E
    qall = _dot(cq_ref[0], wq_ref[0])
    qs_ref[:, :MLA_NOPE] = (qall[:, :MLA_NOPE] * scale).astype(BF16)
    qs_ref[:, MLA_NOPE:] = (_rope(qall[:, MLA_NOPE:], rope_ref[...]) * scale).astype(BF16)
    kv = _dot(ckv_ref[0], wkv_ref[0])
    ks_ref[:, :MLA_NOPE] = kv[:, :MLA_NOPE].astype(BF16)
    ks_ref[:, MLA_NOPE:] = kpe_ref[0]
    vs_ref[:, :MLA_V] = kv[:, MLA_NOPE:].astype(BF16)
    vs_ref[:, MLA_V:] = jnp.ones((L, MXU_WIDTH_V7X - MLA_V), BF16)

    tq = min(Q_TOKENS, L)
    nq = L // tq
    scores = lambda i: _dot_nt(qs_ref[i * tq:(i + 1) * tq, :], ks_ref[...])
    pending = [scores(j) for j in range(min(Q_AHEAD, nq))]
    for i in range(nq):
        if i + Q_AHEAD < nq:
            pending.append(scores(i + Q_AHEAD))
        s = pending.pop(0)
        e = jnp.exp2(s - jnp.max(s, axis=-1, keepdims=True)).astype(BF16)
        o_ext = _dot(e, vs_ref[...])
        o = o_ext[:, :MLA_V] * (1.0 / o_ext[:, MLA_V:MLA_V + 1])
        o_ref[0, i * tq:(i + 1) * tq, :] = o.astype(BF16)


def _mla(cq, ckv, kpe, p):
    B, L, R = cq.shape
    dqk = MLA_NOPE + MLA_ROPE
    seq = lambda n: pl.BlockSpec((1, L, n), lambda b, h: (b, 0, 0))
    per_head = lambda a: pl.BlockSpec((1,) + a.shape[1:], lambda b, h: (h, 0, 0))
    return pl.pallas_call(
        _mla_kernel,
        grid=(B, MLA_HEADS),
        in_specs=[seq(R), seq(ckv.shape[2]), seq(MLA_ROPE), _resident(p['rope'].shape),
                  per_head(p['wq_mla']), per_head(p['wkv_mla'])],
        out_specs=pl.BlockSpec((1, L, MLA_V), lambda b, h: (b, 0, h)),
        out_shape=jax.ShapeDtypeStruct((B, L, MLA_HEADS * MLA_V), BF16),
        scratch_shapes=[pltpu.VMEM((L, dqk), BF16), pltpu.VMEM((L, dqk), BF16),
                        pltpu.VMEM((L, MXU_WIDTH_V7X), BF16)],
        compiler_params=_params(),
        name="mla",
    )(cq, ckv, kpe, p['rope'], p['wq_mla'], p['wkv_mla'])


def _out_kernel(x_ref, oa_ref, ob_ref, ga_ref, gb_ref, woa_ref, wob_ref, wout_ref,
                wgate_ref, wup_ref, wdown_ref, npost_ref, nfpre_ref, nfpost_ref, y_ref):
    tm = x_ref.shape[1]
    parts = [pl.ds(i * (tm // OUT_PARTS), tm // OUT_PARTS) for i in range(OUT_PARTS)]
    ya = [_dot(oa_ref[0, r, :], woa_ref[...]) for r in parts]
    yb = [_dot(ob_ref[0, r, :], wob_ref[...]) for r in parts]
    merged = [(ga_ref[0, r, :].astype(F32) * a + gb_ref[0, r, :].astype(F32) * b).astype(BF16)
              for r, a, b in zip(parts, ya, yb)]
    mo = [_dot(m, wout_ref[...]) for m in merged]
    x1 = [x_ref[0, r, :] + _rms(o, npost_ref[...]) for r, o in zip(parts, mo)]
    h = [_rms(x, nfpre_ref[...]).astype(BF16) for x in x1]
    gate = [_dot(v, wgate_ref[...]) for v in h]
    up = [_dot(v, wup_ref[...]) for v in h]
    act = [(_silu(g) * u).astype(BF16) for g, u in zip(gate, up)]
    down = [_dot(a, wdown_ref[...]) for a in act]
    for r, x, d in zip(parts, x1, down):
        y_ref[0, r, :] = x + _rms(d, nfpost_ref[...])


def _out(x, oa, ob, ga, gb, p):
    B, L, D = x.shape
    tm = min(OUT_TOKENS, L)
    tok = pl.BlockSpec((1, tm, D), lambda b, j: (b, j, 0))
    consts = [p['w_o_gla'], p['w_o_mla'], p['w_out'], p['w_gate'], p['w_up'], p['w_down'],
              p['g_post'], p['g_ffn_pre'], p['g_ffn_post']]
    return pl.pallas_call(
        _out_kernel,
        grid=(B, L // tm),
        in_specs=[tok] * 5 + [_resident(c.shape) for c in consts],
        out_specs=tok,
        out_shape=jax.ShapeDtypeStruct((B, L, D), F32),
        compiler_params=_params(),
        name="merge_ffn",
    )(x, oa, ob, ga, gb, *consts)


def _prepare(L, norm_mix_pre, w_in, gla_wa_fwd, gla_ba_fwd, gla_wa_bwd, gla_ba_bwd, gla_norm,
             w_o_gla, mla_norm_q, w_uq, mla_norm_kv, w_uk, w_uv, w_o_mla, w_out,
             norm_mix_post, norm_ffn_pre, w_gate, w_up, w_down, norm_ffn_post):
    D = w_in.shape[0]
    qk = GLA_HEADS * GLA_DK
    vw = GLA_HEADS * GLA_DV
    rq, rkv = mla_norm_q.shape[0], mla_norm_kv.shape[0]
    widths = (qk, qk, vw, vw, GLA_RANK, GLA_RANK, rq, rkv, MLA_ROPE, D, D)
    offs = np.concatenate([[0], np.cumsum(widths)]).tolist()
    piece = lambda i, j=None: w_in[:, offs[i]:offs[(i if j is None else j) + 1]]
    swap = np.concatenate([np.arange(MLA_ROPE // 2, MLA_ROPE), np.arange(MLA_ROPE // 2)])
    w_kr = piece(8)
    row = lambda a: a.reshape(1, -1).astype(F32)

    inv = ROPE_BASE ** (-jnp.arange(0, MLA_ROPE, 2, dtype=F32) / MLA_ROPE)
    ang = jnp.arange(L, dtype=F32)[:, None] * inv[None, :]
    cos, sin = jnp.cos(ang), jnp.sin(ang)

    uq = w_uq.reshape(rq, MLA_HEADS, MLA_NOPE + MLA_ROPE)
    wq_pe = uq[:, :, MLA_NOPE:]
    wq_mla = jnp.concatenate([uq[:, :, :MLA_NOPE], wq_pe, wq_pe[:, :, swap]], axis=2)
    wkv_mla = jnp.concatenate([w_uk.reshape(rkv, MLA_HEADS, MLA_NOPE),
                               w_uv.reshape(rkv, MLA_HEADS, MLA_V)], axis=2)
    zeros = jnp.zeros((GLA_RANK, qk), F32)
    return dict(
        g_pre=row(norm_mix_pre),
        wq=piece(0).astype(BF16), wk=piece(1).astype(BF16), wv=piece(2).astype(BF16),
        wg=piece(3).astype(BF16), wlat=piece(6, 7).astype(BF16),
        wmisc=jnp.concatenate([w_kr, w_kr[:, swap], piece(4, 5)], axis=1).astype(BF16),
        wga=piece(9).astype(BF16), wgb=piece(10).astype(BF16),
        nq=row(mla_norm_q), nkv=row(mla_norm_kv),
        rope=jnp.concatenate([cos, cos, -sin, sin], axis=1),
        wa=jnp.concatenate([jnp.concatenate([gla_wa_fwd, zeros], axis=1),
                            jnp.concatenate([zeros, gla_wa_bwd], axis=1)], axis=0).astype(BF16),
        ba=jnp.concatenate([row(gla_ba_fwd), row(gla_ba_bwd)], axis=1), gla_norm=row(gla_norm),
        wq_mla=jnp.transpose(wq_mla, (1, 0, 2)).astype(BF16),
        wkv_mla=jnp.transpose(wkv_mla, (1, 0, 2)).astype(BF16),
        w_o_gla=w_o_gla.astype(BF16), w_o_mla=w_o_mla.astype(BF16), w_out=w_out.astype(BF16),
        w_gate=w_gate.astype(BF16), w_up=w_up.astype(BF16), w_down=w_down.astype(BF16),
        g_post=row(norm_mix_post), g_ffn_pre=row(norm_ffn_pre), g_ffn_post=row(norm_ffn_post),
    )


def _layer(x, p):
    q, k, v, g, la_f, la_b, cq, ckv, kpe, ga, gb = _in_proj(x, p)
    oa = _gla(q, k, v, g, la_f, la_b, p)
    ob = _mla(cq, ckv, kpe, p)
    return _out(x, oa, ob, ga, gb, p)


def kernel(x_prompt, x_sample, norm_mix_pre, w_in, gla_wa_fwd, gla_ba_fwd, gla_wa_bwd, gla_ba_bwd, gla_norm, w_o_gla, mla_norm_q, w_uq, mla_norm_kv, w_uk, w_uv, w_o_mla, w_out, norm_mix_post, norm_ffn_pre, w_gate, w_up, w_down, norm_ffn_post):
    assert x_prompt.shape[1] == x_sample.shape[1]
    stacked = (norm_mix_pre, w_in, gla_wa_fwd, gla_ba_fwd, gla_wa_bwd, gla_ba_bwd, gla_norm,
               w_o_gla, mla_norm_q, w_uq, mla_norm_kv, w_uk, w_uv, w_o_mla, w_out,
               norm_mix_post, norm_ffn_pre, w_gate, w_up, w_down, norm_ffn_post)
    y_prompt, y_sample = x_prompt, x_sample
    for layer in range(norm_mix_pre.shape[0]):
        p = _prepare(x_prompt.shape[1], *[a[layer] for a in stacked])
        y_prompt, y_sample = _layer(y_prompt, p), _layer(y_sample, p)
    return (y_prompt, y_sample)
```

```python
import jax
import jax.numpy as jnp
import numpy as np
from jax import lax
from jax.experimental import pallas as pl
from jax.experimental.pallas import tpu as pltpu

F32 = jnp.float32
BF16 = jnp.bfloat16

EPS = 1e-6
GLA_HEADS = 4
GLA_DK = 128
GLA_DV = 256
GLA_RANK = 16
GLA_TEMP = 16.0
GLA_CHUNK = 128
MLA_HEADS = 8
MLA_NOPE = 128
MLA_ROPE = 64
MLA_V = 128
ROPE_BASE = 10000.0

IN_TOKENS = 512
IN_PARTS = 2
OUT_TOKENS = 512
OUT_PARTS = 2
Q_TOKENS = 256
Q_AHEAD = 1
MXU_WIDTH_V7X = 256
LOG2_E = 1.4426950408889634
GLA_UNROLL = 16
INTRA_GROUP = 16
NORM_ROWS = 256
VMEM_LIMIT_BYTES = 56 * 1024 * 1024


def _rms(xf, g):
    ms = jnp.mean(xf * xf, axis=-1, keepdims=True)
    return xf * lax.rsqrt(ms + EPS) * g


def _silu(x):
    h = 0.5 * x
    return h + h * jnp.tanh(h)


def _dot(a, b):
    return jnp.dot(a, b, preferred_element_type=F32)


def _dot_nt(a, b):
    return lax.dot_general(a, b, (((1,), (1,)), ((), ())), preferred_element_type=F32)


def _rope(x2, table):
    y = x2 * table
    return (y + pltpu.roll(y, MLA_ROPE, axis=1))[:, :MLA_ROPE]


def _params():
    return pltpu.CompilerParams(
        dimension_semantics=("arbitrary", "arbitrary"),
        vmem_limit_bytes=VMEM_LIMIT_BYTES,
    )


def _resident(shape):
    nd = len(shape)
    return pl.BlockSpec(shape, lambda *_: (0,) * nd, pipeline_mode=pl.Buffered(1))


def _in_proj_kernel(x_ref, g_ref, rope_ref, nq_ref, nkv_ref,
                    wq_ref, wk_ref, wv_ref, wg_ref, wlat_ref, wmisc_ref, wga_ref, wgb_ref,
                    wa_ref, ba_ref,
                    q_out, k_out, v_out, g_out, laf_out, lab_out, cq_out, ckv_out, kpe_out,
                    ga_out, gb_out):
    tm = x_ref.shape[1]
    parts = [pl.ds(i * (tm // IN_PARTS), tm // IN_PARTS) for i in range(IN_PARTS)]
    h = [_rms(x_ref[0, r, :], g_ref[...]).astype(BF16) for r in parts]

    def project(w_ref, out, epilogue=lambda y: y):
        for r, y in zip(parts, [_dot(v, w_ref[...]) for v in h]):
            out[0, r, :] = epilogue(y).astype(out.dtype)

    misc = [_dot(v, wmisc_ref[...]) for v in h]
    for r, m in zip(parts, misc):
        kpe_out[0, r, :] = _rope(m[:, :2 * MLA_ROPE], rope_ref[r, :]).astype(BF16)

    lat_cols = slice(2 * MLA_ROPE, 2 * MLA_ROPE + 2 * GLA_RANK)
    zs = [_dot(m[:, lat_cols].astype(BF16), wa_ref[...]) + ba_ref[...] for m in misc]
    qk = GLA_HEADS * GLA_DK
    for r, z in zip(parts, zs):
        la = (jnp.minimum(z, 0.0) - jnp.log(1.0 + jnp.exp(-jnp.abs(z)))) * (LOG2_E / GLA_TEMP)
        laf_out[0, r, :] = la[:, :qk]
        lab_out[0, r, :] = la[:, qk:]

    project(wv_ref, v_out)
    project(wg_ref, g_out)
    project(wga_ref, ga_out, jax.nn.sigmoid)
    project(wq_ref, q_out)
    project(wgb_ref, gb_out, jax.nn.sigmoid)
    rq = nq_ref.shape[1]
    for r, lat in zip(parts, [_dot(v, wlat_ref[...]) for v in h]):
        cq_out[0, r, :] = _rms(lat[:, :rq], nq_ref[...]).astype(BF16)
        ckv_out[0, r, :] = _rms(lat[:, rq:], nkv_ref[...]).astype(BF16)
    project(wk_ref, k_out)


def _in_proj(x, p):
    B, L, D = x.shape
    tm = min(IN_TOKENS, L)
    grid = (B, L // tm)
    tok = lambda n, dt=BF16: (pl.BlockSpec((1, tm, n), lambda b, j: (b, j, 0)),
                              jax.ShapeDtypeStruct((B, L, n), dt))
    outs = [tok(p['wq'].shape[1]), tok(p['wk'].shape[1]), tok(p['wv'].shape[1]),
            tok(p['wg'].shape[1]), tok(p['wq'].shape[1], F32), tok(p['wq'].shape[1], F32),
            tok(p['nq'].shape[1]), tok(p['nkv'].shape[1]), tok(MLA_ROPE), tok(D), tok(D)]
    weights = [p['wq'], p['wk'], p['wv'], p['wg'], p['wlat'], p['wmisc'], p['wga'], p['wgb'],
               p['wa'], p['ba']]
    return pl.pallas_call(
        _in_proj_kernel,
        grid=grid,
        in_specs=[pl.BlockSpec((1, tm, D), lambda b, j: (b, j, 0)),
                  _resident(p['g_pre'].shape),
                  pl.BlockSpec((tm, 2 * MLA_ROPE), lambda b, j: (j, 0)),
                  _resident(p['nq'].shape), _resident(p['nkv'].shape)]
                 + [_resident(w.shape) for w in weights],
        out_specs=[o[0] for o in outs],
        out_shape=[o[1] for o in outs],
        compiler_params=_params(),
        name="in_proj",
    )(x, p['g_pre'], p['rope'], p['nq'], p['nkv'], *weights)


def _gla_kernel(q_ref, k_ref, v_ref, g_ref, laf_ref, lab_ref,
                gn_ref, o_ref, mf_ref, mb_ref, sf_ref, sb_ref, bf_ref, bb_ref,
                qif_ref, qib_ref, uf_ref, ub_ref, dtf_ref, dtb_ref, *rest):
    L = q_ref.shape[1]
    C = GLA_CHUNK
    nc = L // C
    nr = min(NORM_ROWS, L)
    mid = C // 2 - 1
    scale = GLA_DK ** -0.5
    a_refs, oacc_ref = rest[:-1], rest[-1]
    group = len(a_refs)
    assert nc % group == 0

    row = lax.broadcasted_iota(jnp.int32, (C, C), 0)
    col = lax.broadcasted_iota(jnp.int32, (C, C), 1)
    mf_ref[...] = jnp.where(col <= row, 1.0, 0.0).astype(BF16)
    mb_ref[...] = jnp.where(col >= row, 1.0, 0.0).astype(BF16)

    def cumsum(c, carry):
        rows = pl.ds(pl.multiple_of(c * C, C), C)
        for m_ref, la_ref, b_ref in ((mf_ref, laf_ref, bf_ref), (mb_ref, lab_ref, bb_ref)):
            la = la_ref[0, rows, :]
            hi = la.astype(BF16)
            lo = (la - hi.astype(F32)).astype(BF16)
            cs = _dot(m_ref[...], jnp.concatenate([hi, lo], axis=1))
            b_ref[rows, :] = cs[:, :GLA_DK] + cs[:, GLA_DK:]
        return carry

    lax.fori_loop(0, nc, cumsum, 0, unroll=GLA_UNROLL)

    dirs = ((bf_ref, qif_ref, uf_ref, dtf_ref, C - 1, mid, True),
            (bb_ref, qib_ref, ub_ref, dtb_ref, 0, mid + 1, False))

    def chunk_rows(c):
        return pl.ds(c * C if isinstance(c, int) else pl.multiple_of(c * C, C), C)

    def intra_out(grp):
        for slot, a_ref in enumerate(a_refs):
            rows = chunk_rows(group * grp + slot)
            oacc_ref[rows, :] = _dot(a_ref[chunk_rows(grp), :], v_ref[0, rows, :])

    lagged = nc > group
    if lagged:
        for a_ref in a_refs:
            a_ref[:C, :] = jnp.zeros((C, C), BF16)

    def intra(grp, carry):
        if lagged:
            intra_out(jnp.maximum(grp - 1, 0))
        for slot, a_ref in enumerate(a_refs):
            rows = chunk_rows(group * grp + slot)
            q = q_ref[0, rows, :].astype(F32) * scale
            k = k_ref[0, rows, :].astype(F32)
            v = v_ref[0, rows, :]
            a_sum = None
            for b_ref, qi_ref, u_ref, dt_ref, tot_row, ref_row, fwd in dirs:
                b_incl = b_ref[rows, :]
                b_mid = b_incl[ref_row:ref_row + 1, :]
                b_tot = b_incl[tot_row:tot_row + 1, :]
                b_rel = b_incl - b_mid
                q_rel = q * jnp.exp2(b_rel)
                k_rel = k * jnp.exp2(-b_rel)
                qi_ref[rows, :] = (q_rel * jnp.exp2(b_mid)).astype(BF16)
                k_end = k_rel * jnp.exp2(b_tot - b_mid)
                mask = (row >= col) if fwd else (row <= col)
                a = jnp.where(mask, _dot_nt(q_rel.astype(BF16), k_rel.astype(BF16)), 0.0)
                a_sum = a if a_sum is None else a_sum + a
                u_ref[rows, :] = _dot(k_end.T.astype(BF16), v)
                dt_ref[rows, :] = jnp.broadcast_to(jnp.exp2(b_tot), (C, GLA_DK)).T
            a_ref[chunk_rows(grp), :] = a_sum.astype(BF16)
        return carry

    if lagged:
        lax.fori_loop(0, nc // group, intra, 0)
    else:
        intra(0, 0)
    intra_out(nc // group - 1)

    sf_ref[...] = jnp.zeros_like(sf_ref)
    sb_ref[...] = jnp.zeros_like(sb_ref)

    def scan(i, carry):
        for s_ref, qi_ref, u_ref, dt_ref, c in ((sf_ref, qif_ref, uf_ref, dtf_ref, i),
                                                (sb_ref, qib_ref, ub_ref, dtb_ref, nc - 1 - i)):
            rows = pl.ds(pl.multiple_of(c * C, C), C)
            s = s_ref[...]
            oacc_ref[rows, :] += _dot(qi_ref[rows, :], s.astype(BF16))
            dt = dt_ref[rows, :]
            s_ref[...] = s * jnp.concatenate([dt] * (GLA_DV // C), axis=1) + u_ref[rows, :]
        return carry

    lax.fori_loop(0, nc, scan, 0, unroll=GLA_UNROLL)

    for r in range(L // nr):
        rows = pl.ds(r * nr, nr)
        y = _rms(oacc_ref[rows, :], gn_ref[...])
        g = g_ref[0, rows, :].astype(F32)
        o_ref[0, rows, :] = (y * _silu(g)).astype(BF16)


def _gla(q, k, v, g, la_f, la_b, p):
    B, L, _ = q.shape
    C = GLA_CHUNK
    group = min(INTRA_GROUP, L // C)
    head = lambda n: pl.BlockSpec((1, L, n), lambda b, h: (b, 0, h))
    return pl.pallas_call(
        _gla_kernel,
        grid=(B, GLA_HEADS),
        in_specs=[head(GLA_DK), head(GLA_DK), head(GLA_DV), head(GLA_DV),
                  head(GLA_DK), head(GLA_DK),
                  pl.BlockSpec((1, GLA_DV), lambda b, h: (0, 0))],
        out_specs=head(GLA_DV),
        out_shape=jax.ShapeDtypeStruct((B, L, GLA_HEADS * GLA_DV), BF16),
        scratch_shapes=[pltpu.VMEM((C, C), BF16), pltpu.VMEM((C, C), BF16),
                        pltpu.VMEM((GLA_DK, GLA_DV), F32), pltpu.VMEM((GLA_DK, GLA_DV), F32),
                        pltpu.VMEM((L, GLA_DK), F32), pltpu.VMEM((L, GLA_DK), F32),
                        pltpu.VMEM((L, GLA_DK), BF16), pltpu.VMEM((L, GLA_DK), BF16),
                        pltpu.VMEM((L, GLA_DV), F32), pltpu.VMEM((L, GLA_DV), F32),
                        pltpu.VMEM((L, GLA_DK), F32), pltpu.VMEM((L, GLA_DK), F32),
                        *[pltpu.VMEM((L // group, C), BF16) for _ in range(group)],
                        pltpu.VMEM((L, GLA_DV), F32)],
        compiler_params=_params(),
        name="gla",
    )(q, k, v, g, la_f, la_b, p['gla_norm'])


def _mla_kernel(cq_ref, ckv_ref, kpe_ref, rope_ref, wq_ref, wkv_ref, o_ref,
                qs_ref, ks_ref, vs_ref):
    L = cq_ref.shape[1]
    scale = (MLA_NOPE + MLA_ROPE) ** -0.5 * LOG2_E
    qall = _dot(cq_ref[0], wq_ref[0])
    qs_ref[:, :MLA_NOPE] = (qall[:, :MLA_NOPE] * scale).astype(BF16)
    qs_ref[:, MLA_NOPE:] = (_rope(qall[:, MLA_NOPE:], rope_ref[...]) * scale).astype(BF16)
    kv = _dot(ckv_ref[0], wkv_ref[0])
    ks_ref[:, :MLA_NOPE] = kv[:, :MLA_NOPE].astype(BF16)
    ks_ref[:, MLA_NOPE:] = kpe_ref[0]
    vs_ref[:, :MLA_V] = kv[:, MLA_NOPE:].astype(BF16)
    vs_ref[:, MLA_V:] = jnp.ones((L, MXU_WIDTH_V7X - MLA_V), BF16)

    tq = min(Q_TOKENS, L)
    nq = L // tq
    scores = lambda i: _dot_nt(qs_ref[i * tq:(i + 1) * tq, :], ks_ref[...])
    pending = [scores(j) for j in range(min(Q_AHEAD, nq))]
    for i in range(nq):
        if i + Q_AHEAD < nq:
            pending.append(scores(i + Q_AHEAD))
        s = pending.pop(0)
        e = jnp.exp2(s - jnp.max(s, axis=-1, keepdims=True)).astype(BF16)
        o_ext = _dot(e, vs_ref[...])
        o = o_ext[:, :MLA_V] * (1.0 / o_ext[:, MLA_V:MLA_V + 1])
        o_ref[0, i * tq:(i + 1) * tq, :] = o.astype(BF16)


def _mla(cq, ckv, kpe, p):
    B, L, R = cq.shape
    dqk = MLA_NOPE + MLA_ROPE
    seq = lambda n: pl.BlockSpec((1, L, n), lambda b, h: (b, 0, 0))
    per_head = lambda a: pl.BlockSpec((1,) + a.shape[1:], lambda b, h: (h, 0, 0))
    return pl.pallas_call(
        _mla_kernel,
        grid=(B, MLA_HEADS),
        in_specs=[seq(R), seq(ckv.shape[2]), seq(MLA_ROPE), _resident(p['rope'].shape),
                  per_head(p['wq_mla']), per_head(p['wkv_mla'])],
        out_specs=pl.BlockSpec((1, L, MLA_V), lambda b, h: (b, 0, h)),
        out_shape=jax.ShapeDtypeStruct((B, L, MLA_HEADS * MLA_V), BF16),
        scratch_shapes=[pltpu.VMEM((L, dqk), BF16), pltpu.VMEM((L, dqk), BF16),
                        pltpu.VMEM((L, MXU_WIDTH_V7X), BF16)],
        compiler_params=_params(),
        name="mla",
    )(cq, ckv, kpe, p['rope'], p['wq_mla'], p['wkv_mla'])


def _out_kernel(x_ref, oa_ref, ob_ref, ga_ref, gb_ref, woa_ref, wob_ref, wout_ref,
                wgate_ref, wup_ref, wdown_ref, npost_ref, nfpre_ref, nfpost_ref, y_ref):
    tm = x_ref.shape[1]
    parts = [pl.ds(i * (tm // OUT_PARTS), tm // OUT_PARTS) for i in range(OUT_PARTS)]
    ya = [_dot(oa_ref[0, r, :], woa_ref[...]) for r in parts]
    yb = [_dot(ob_ref[0, r, :], wob_ref[...]) for r in parts]
    merged = [(ga_ref[0, r, :].astype(F32) * a + gb_ref[0, r, :].astype(F32) * b).astype(BF16)
              for r, a, b in zip(parts, ya, yb)]
    mo = [_dot(m, wout_ref[...]) for m in merged]
    x1 = [x_ref[0, r, :] + _rms(o, npost_ref[...]) for r, o in zip(parts, mo)]
    h = [_rms(x, nfpre_ref[...]).astype(BF16) for x in x1]
    gate = [_dot(v, wgate_ref[...]) for v in h]
    up = [_dot(v, wup_ref[...]) for v in h]
    act = [(_silu(g) * u).astype(BF16) for g, u in zip(gate, up)]
    down = [_dot(a, wdown_ref[...]) for a in act]
    for r, x, d in zip(parts, x1, down):
        y_ref[0, r, :] = x + _rms(d, nfpost_ref[...])


def _out(x, oa, ob, ga, gb, p):
    B, L, D = x.shape
    tm = min(OUT_TOKENS, L)
    tok = pl.BlockSpec((1, tm, D), lambda b, j: (b, j, 0))
    consts = [p['w_o_gla'], p['w_o_mla'], p['w_out'], p['w_gate'], p['w_up'], p['w_down'],
              p['g_post'], p['g_ffn_pre'], p['g_ffn_post']]
    return pl.pallas_call(
        _out_kernel,
        grid=(B, L // tm),
        in_specs=[tok] * 5 + [_resident(c.shape) for c in consts],
        out_specs=tok,
        out_shape=jax.ShapeDtypeStruct((B, L, D), F32),
        compiler_params=_params(),
        name="merge_ffn",
    )(x, oa, ob, ga, gb, *consts)


def _prepare(L, norm_mix_pre, w_in, gla_wa_fwd, gla_ba_fwd, gla_wa_bwd, gla_ba_bwd, gla_norm,
             w_o_gla, mla_norm_q, w_uq, mla_norm_kv, w_uk, w_uv, w_o_mla, w_out,
             norm_mix_post, norm_ffn_pre, w_gate, w_up, w_down, norm_ffn_post):
    D = w_in.shape[0]
    qk = GLA_HEADS * GLA_DK
    vw = GLA_HEADS * GLA_DV
    rq, rkv = mla_norm_q.shape[0], mla_norm_kv.shape[0]
    widths = (qk, qk, vw, vw, GLA_RANK, GLA_RANK, rq, rkv, MLA_ROPE, D, D)
    offs = np.concatenate([[0], np.cumsum(widths)]).tolist()
    piece = lambda i, j=None: w_in[:, offs[i]:offs[(i if j is None else j) + 1]]
    swap = np.concatenate([np.arange(MLA_ROPE // 2, MLA_ROPE), np.arange(MLA_ROPE // 2)])
    w_kr = piece(8)
    row = lambda a: a.reshape(1, -1).astype(F32)

    inv = ROPE_BASE ** (-jnp.arange(0, MLA_ROPE, 2, dtype=F32) / MLA_ROPE)
    ang = jnp.arange(L, dtype=F32)[:, None] * inv[None, :]
    cos, sin = jnp.cos(ang), jnp.sin(ang)

    uq = w_uq.reshape(rq, MLA_HEADS, MLA_NOPE + MLA_ROPE)
    wq_pe = uq[:, :, MLA_NOPE:]
    wq_mla = jnp.concatenate([uq[:, :, :MLA_NOPE], wq_pe, wq_pe[:, :, swap]], axis=2)
    wkv_mla = jnp.concatenate([w_uk.reshape(rkv, MLA_HEADS, MLA_NOPE),
                               w_uv.reshape(rkv, MLA_HEADS, MLA_V)], axis=2)
    zeros = jnp.zeros((GLA_RANK, qk), F32)
    return dict(
        g_pre=row(norm_mix_pre),
        wq=piece(0).astype(BF16), wk=piece(1).astype(BF16), wv=piece(2).astype(BF16),
        wg=piece(3).astype(BF16), wlat=piece(6, 7).astype(BF16),
        wmisc=jnp.concatenate([w_kr, w_kr[:, swap], piece(4, 5)], axis=1).astype(BF16),
        wga=piece(9).astype(BF16), wgb=piece(10).astype(BF16),
        nq=row(mla_norm_q), nkv=row(mla_norm_kv),
        rope=jnp.concatenate([cos, cos, -sin, sin], axis=1),
        wa=jnp.concatenate([jnp.concatenate([gla_wa_fwd, zeros], axis=1),
                            jnp.concatenate([zeros, gla_wa_bwd], axis=1)], axis=0).astype(BF16),
        ba=jnp.concatenate([row(gla_ba_fwd), row(gla_ba_bwd)], axis=1), gla_norm=row(gla_norm),
        wq_mla=jnp.transpose(wq_mla, (1, 0, 2)).astype(BF16),
        wkv_mla=jnp.transpose(wkv_mla, (1, 0, 2)).astype(BF16),
        w_o_gla=w_o_gla.astype(BF16), w_o_mla=w_o_mla.astype(BF16), w_out=w_out.astype(BF16),
        w_gate=w_gate.astype(BF16), w_up=w_up.astype(BF16), w_down=w_down.astype(BF16),
        g_post=row(norm_mix_post), g_ffn_pre=row(norm_ffn_pre), g_ffn_post=row(norm_ffn_post),
    )


def _layer(x, p):
    q, k, v, g, la_f, la_b, cq, ckv, kpe, ga, gb = _in_proj(x, p)
    oa = _gla(q, k, v, g, la_f, la_b, p)
    ob = _mla(cq, ckv, kpe, p)
    return _out(x, oa, ob, ga, gb, p)


def kernel(x_prompt, x_sample, norm_mix_pre, w_in, gla_wa_fwd, gla_ba_fwd, gla_wa_bwd, gla_ba_bwd, gla_norm, w_o_gla, mla_norm_q, w_uq, mla_norm_kv, w_uk, w_uv, w_o_mla, w_out, norm_mix_post, norm_ffn_pre, w_gate, w_up, w_down, norm_ffn_post):
    assert x_prompt.shape[1] == x_sample.shape[1]
    stacked = (norm_mix_pre, w_in, gla_wa_fwd, gla_ba_fwd, gla_wa_bwd, gla_ba_bwd, gla_norm,
               w_o_gla, mla_norm_q, w_uq, mla_norm_kv, w_uk, w_uv, w_o_mla, w_out,
               norm_mix_post, norm_ffn_pre, w_gate, w_up, w_down, norm_ffn_post)
    y_prompt, y_sample = x_prompt, x_sample
    for layer in range(norm_mix_pre.shape[0]):
        p = _prepare(x_prompt.shape[1], *[a[layer] for a in stacked])
        y_prompt, y_sample = _layer(y_prompt, p), _layer(y_sample, p)
    return (y_prompt, y_sample)
```

```python
import jax
import jax.numpy as jnp
import numpy as np
from jax import lax
from jax.experimental import pallas as pl
from jax.experimental.pallas import tpu as pltpu

F32 = jnp.float32
BF16 = jnp.bfloat16

EPS = 1e-6
GLA_HEADS = 4
GLA_DK = 128
GLA_DV = 256
GLA_RANK = 16
GLA_TEMP = 16.0
GLA_CHUNK = 128
MLA_HEADS = 8
MLA_NOPE = 128
MLA_ROPE = 64
MLA_V = 128
ROPE_BASE = 10000.0

IN_TOKENS = 512
IN_PARTS = 2
OUT_TOKENS = 512
OUT_PARTS = 2
Q_TOKENS = 256
Q_AHEAD = 1
MXU_WIDTH_V7X = 256
LOG2_E = 1.4426950408889634
GLA_UNROLL = 16
INTRA_GROUP = 16
NORM_ROWS = 256
VMEM_LIMIT_BYTES = 56 * 1024 * 1024


def _rms(xf, g):
    ms = jnp.mean(xf * xf, axis=-1, keepdims=True)
    return xf * lax.rsqrt(ms + EPS) * g


def _silu(x):
    h = 0.5 * x
    return h + h * jnp.tanh(h)


def _dot(a, b):
    return jnp.dot(a, b, preferred_element_type=F32)


def _dot_nt(a, b):
    return lax.dot_general(a, b, (((1,), (1,)), ((), ())), preferred_element_type=F32)


def _rope(x2, table):
    y = x2 * table
    y = y + pltpu.roll(y, MLA_ROPE, axis=1)
    lane = lax.broadcasted_iota(jnp.int32, y.shape, 1)
    return jnp.where(lane < MLA_ROPE, y, 0.0)


def _params():
    return pltpu.CompilerParams(
        dimension_semantics=("arbitrary", "arbitrary"),
        vmem_limit_bytes=VMEM_LIMIT_BYTES,
    )


def _resident(shape):
    nd = len(shape)
    return pl.BlockSpec(shape, lambda *_: (0,) * nd, pipeline_mode=pl.Buffered(1))


def _in_proj_kernel(x_ref, g_ref, rope_ref, nq_ref, nkv_ref,
                    wq_ref, wk_ref, wv_ref, wg_ref, wlat_ref, wmisc_ref, wga_ref, wgb_ref,
                    wa_ref, ba_ref,
                    q_out, k_out, v_out, g_out, laf_out, lab_out, cq_out, ckv_out, kpe_out,
                    ga_out, gb_out):
    tm = x_ref.shape[1]
    parts = [pl.ds(i * (tm // IN_PARTS), tm // IN_PARTS) for i in range(IN_PARTS)]
    h = [_rms(x_ref[0, r, :], g_ref[...]).astype(BF16) for r in parts]

    def project(w_ref, out, epilogue=lambda y: y):
        for r, y in zip(parts, [_dot(v, w_ref[...]) for v in h]):
            out[0, r, :] = epilogue(y).astype(out.dtype)

    misc = [_dot(v, wmisc_ref[...]) for v in h]
    for r, m in zip(parts, misc):
        kpe_out[0, r, :] = _rope(m[:, :2 * MLA_ROPE], rope_ref[r, :]).astype(BF16)

    lat_cols = slice(2 * MLA_ROPE, 2 * MLA_ROPE + 2 * GLA_RANK)
    zs = [_dot(m[:, lat_cols].astype(BF16), wa_ref[...]) + ba_ref[...] for m in misc]
    qk = GLA_HEADS * GLA_DK
    for r, z in zip(parts, zs):
        la = (jnp.minimum(z, 0.0) - jnp.log(1.0 + jnp.exp(-jnp.abs(z)))) * (LOG2_E / GLA_TEMP)
        laf_out[0, r, :] = la[:, :qk]
        lab_out[0, r, :] = la[:, qk:]

    project(wv_ref, v_out)
    project(wg_ref, g_out)
    project(wga_ref, ga_out, jax.nn.sigmoid)
    project(wq_ref, q_out)
    project(wgb_ref, gb_out, jax.nn.sigmoid)
    rq = nq_ref.shape[1]
    for r, lat in zip(parts, [_dot(v, wlat_ref[...]) for v in h]):
        cq_out[0, r, :] = _rms(lat[:, :rq], nq_ref[...]).astype(BF16)
        ckv_out[0, r, :] = _rms(lat[:, rq:], nkv_ref[...]).astype(BF16)
    project(wk_ref, k_out)


def _in_proj(x, p):
    B, L, D = x.shape
    tm = min(IN_TOKENS, L)
    grid = (B, L // tm)
    tok = lambda n, dt=BF16: (pl.BlockSpec((1, tm, n), lambda b, j: (b, j, 0)),
                              jax.ShapeDtypeStruct((B, L, n), dt))
    outs = [tok(p['wq'].shape[1]), tok(p['wk'].shape[1]), tok(p['wv'].shape[1]),
            tok(p['wg'].shape[1]), tok(p['wq'].shape[1], F32), tok(p['wq'].shape[1], F32),
            tok(p['nq'].shape[1]), tok(p['nkv'].shape[1]), tok(2 * MLA_ROPE), tok(D), tok(D)]
    weights = [p['wq'], p['wk'], p['wv'], p['wg'], p['wlat'], p['wmisc'], p['wga'], p['wgb'],
               p['wa'], p['ba']]
    return pl.pallas_call(
        _in_proj_kernel,
        grid=grid,
        in_specs=[pl.BlockSpec((1, tm, D), lambda b, j: (b, j, 0)),
                  _resident(p['g_pre'].shape),
                  pl.BlockSpec((tm, 2 * MLA_ROPE), lambda b, j: (j, 0)),
                  _resident(p['nq'].shape), _resident(p['nkv'].shape)]
                 + [_resident(w.shape) for w in weights],
        out_specs=[o[0] for o in outs],
        out_shape=[o[1] for o in outs],
        compiler_params=_params(),
        name="in_proj",
    )(x, p['g_pre'], p['rope'], p['nq'], p['nkv'], *weights)


def _gla_kernel(q_ref, k_ref, v_ref, g_ref, laf_ref, lab_ref,
                gn_ref, o_ref, mf_ref, mb_ref, sf_ref, sb_ref, bf_ref, bb_ref,
                qif_ref, qib_ref, uf_ref, ub_ref, dtf_ref, dtb_ref, *rest):
    L = q_ref.shape[1]
    C = GLA_CHUNK
    nc = L // C
    nr = min(NORM_ROWS, L)
    mid = C // 2 - 1
    scale = GLA_DK ** -0.5
    a_refs, oacc_ref = rest[:-1], rest[-1]
    group = len(a_refs)
    assert nc % group == 0

    row = lax.broadcasted_iota(jnp.int32, (C, C), 0)
    col = lax.broadcasted_iota(jnp.int32, (C, C), 1)
    mf_ref[...] = jnp.where(col <= row, 1.0, 0.0).astype(BF16)
    mb_ref[...] = jnp.where(col >= row, 1.0, 0.0).astype(BF16)

    def cumsum(c, carry):
        rows = pl.ds(pl.multiple_of(c * C, C), C)
        for m_ref, la_ref, b_ref in ((mf_ref, laf_ref, bf_ref), (mb_ref, lab_ref, bb_ref)):
            la = la_ref[0, rows, :]
            hi = la.astype(BF16)
            lo = (la - hi.astype(F32)).astype(BF16)
            cs = _dot(m_ref[...], jnp.concatenate([hi, lo], axis=1))
            b_ref[rows, :] = cs[:, :GLA_DK] + cs[:, GLA_DK:]
        return carry

    lax.fori_loop(0, nc, cumsum, 0, unroll=GLA_UNROLL)

    dirs = ((bf_ref, qif_ref, uf_ref, dtf_ref, C - 1, mid, True),
            (bb_ref, qib_ref, ub_ref, dtb_ref, 0, mid + 1, False))

    def chunk_rows(c):
        return pl.ds(c * C if isinstance(c, int) else pl.multiple_of(c * C, C), C)

    def intra_out(grp):
        for slot, a_ref in enumerate(a_refs):
            rows = chunk_rows(group * grp + slot)
            oacc_ref[rows, :] = _dot(a_ref[chunk_rows(grp), :], v_ref[0, rows, :])

    lagged = nc > group
    if lagged:
        for a_ref in a_refs:
            a_ref[:C, :] = jnp.zeros((C, C), BF16)

    def intra(grp, carry):
        if lagged:
            intra_out(jnp.maximum(grp - 1, 0))
        for slot, a_ref in enumerate(a_refs):
            rows = chunk_rows(group * grp + slot)
            q = q_ref[0, rows, :].astype(F32) * scale
            k = k_ref[0, rows, :].astype(F32)
            v = v_ref[0, rows, :]
            a_sum = None
            for b_ref, qi_ref, u_ref, dt_ref, tot_row, ref_row, fwd in dirs:
                b_incl = b_ref[rows, :]
                b_mid = b_incl[ref_row:ref_row + 1, :]
                b_tot = b_incl[tot_row:tot_row + 1, :]
                b_rel = b_incl - b_mid
                q_rel = q * jnp.exp2(b_rel)
                k_rel = k * jnp.exp2(-b_rel)
                qi_ref[rows, :] = (q_rel * jnp.exp2(b_mid)).astype(BF16)
                k_end = k_rel * jnp.exp2(b_tot - b_mid)
                mask = (row >= col) if fwd else (row <= col)
                a = jnp.where(mask, _dot_nt(q_rel.astype(BF16), k_rel.astype(BF16)), 0.0)
                a_sum = a if a_sum is None else a_sum + a
                u_ref[rows, :] = _dot(k_end.T.astype(BF16), v)
                dt_ref[rows, :] = jnp.broadcast_to(jnp.exp2(b_tot), (C, GLA_DK)).T
            a_ref[chunk_rows(grp), :] = a_sum.astype(BF16)
        return carry

    if lagged:
        lax.fori_loop(0, nc // group, intra, 0)
    else:
        intra(0, 0)
    intra_out(nc // group - 1)

    sf_ref[...] = jnp.zeros_like(sf_ref)
    sb_ref[...] = jnp.zeros_like(sb_ref)

    def scan(i, carry):
        for s_ref, qi_ref, u_ref, dt_ref, c in ((sf_ref, qif_ref, uf_ref, dtf_ref, i),
                                                (sb_ref, qib_ref, ub_ref, dtb_ref, nc - 1 - i)):
            rows = pl.ds(pl.multiple_of(c * C, C), C)
            s = s_ref[...]
            oacc_ref[rows, :] += _dot(qi_ref[rows, :], s.astype(BF16))
            dt = dt_ref[rows, :]
            s_ref[...] = s * jnp.concatenate([dt] * (GLA_DV // C), axis=1) + u_ref[rows, :]
        return carry

    lax.fori_loop(0, nc, scan, 0, unroll=GLA_UNROLL)

    for r in range(L // nr):
        rows = pl.ds(r * nr, nr)
        y = _rms(oacc_ref[rows, :], gn_ref[...])
        g = g_ref[0, rows, :].astype(F32)
        o_ref[0, rows, :] = (y * _silu(g)).astype(BF16)


def _gla(q, k, v, g, la_f, la_b, p):
    B, L, _ = q.shape
    C = GLA_CHUNK
    group = min(INTRA_GROUP, L // C)
    head = lambda n: pl.BlockSpec((1, L, n), lambda b, h: (b, 0, h))
    return pl.pallas_call(
        _gla_kernel,
        grid=(B, GLA_HEADS),
        in_specs=[head(GLA_DK), head(GLA_DK), head(GLA_DV), head(GLA_DV),
                  head(GLA_DK), head(GLA_DK),
                  pl.BlockSpec((1, GLA_DV), lambda b, h: (0, 0))],
        out_specs=head(GLA_DV),
        out_shape=jax.ShapeDtypeStruct((B, L, GLA_HEADS * GLA_DV), BF16),
        scratch_shapes=[pltpu.VMEM((C, C), BF16), pltpu.VMEM((C, C), BF16),
                        pltpu.VMEM((GLA_DK, GLA_DV), F32), pltpu.VMEM((GLA_DK, GLA_DV), F32),
                        pltpu.VMEM((L, GLA_DK), F32), pltpu.VMEM((L, GLA_DK), F32),
                        pltpu.VMEM((L, GLA_DK), BF16), pltpu.VMEM((L, GLA_DK), BF16),
                        pltpu.VMEM((L, GLA_DV), F32), pltpu.VMEM((L, GLA_DV), F32),
                        pltpu.VMEM((L, GLA_DK), F32), pltpu.VMEM((L, GLA_DK), F32),
                        *[pltpu.VMEM((L // group, C), BF16) for _ in range(group)],
                        pltpu.VMEM((L, GLA_DV), F32)],
        compiler_params=_params(),
        name="gla",
    )(q, k, v, g, la_f, la_b, p['gla_norm'])


def _mla_kernel(cq_ref, ckv_ref, kpe_ref, rope_ref, wq_ref, wkv_ref, o_ref,
                qs_ref, kt_ref, vs_ref):
    L = cq_ref.shape[1]
    scale = (MLA_NOPE + MLA_ROPE) ** -0.5 * LOG2_E
    qall = _dot(cq_ref[0], wq_ref[0])
    qs_ref[:, :MLA_NOPE] = (qall[:, :MLA_NOPE] * scale).astype(BF16)
    qs_ref[:, MLA_NOPE:] = (_rope(qall[:, MLA_NOPE:], rope_ref[...]) * scale).astype(BF16)
    kv = _dot(ckv_ref[0], wkv_ref[0])
    kt_ref[:MLA_NOPE, :] = kv[:, :MLA_NOPE].T.astype(BF16)
    kt_ref[MLA_NOPE:, :] = kpe_ref[0].astype(F32).T.astype(BF16)
    vs_ref[:, :MLA_V] = kv[:, MLA_NOPE:].astype(BF16)
    vs_ref[:, MLA_V:] = jnp.ones((L, MXU_WIDTH_V7X - MLA_V), BF16)

    tq = min(Q_TOKENS, L)
    nq = L // tq
    scores = lambda i: _dot(qs_ref[i * tq:(i + 1) * tq, :], kt_ref[...])
    pending = [scores(j) for j in range(min(Q_AHEAD, nq))]
    for i in range(nq):
        if i + Q_AHEAD < nq:
            pending.append(scores(i + Q_AHEAD))
        s = pending.pop(0)
        e = jnp.exp2(s - jnp.max(s, axis=-1, keepdims=True)).astype(BF16)
        o_ext = _dot(e, vs_ref[...])
        o = o_ext[:, :MLA_V] * (1.0 / o_ext[:, MLA_V:MLA_V + 1])
        o_ref[0, i * tq:(i + 1) * tq, :] = o.astype(BF16)


def _mla(cq, ckv, kpe, p):
    B, L, R = cq.shape
    dqk = MLA_NOPE + 2 * MLA_ROPE
    seq = lambda n: pl.BlockSpec((1, L, n), lambda b, h: (b, 0, 0))
    per_head = lambda a: pl.BlockSpec((1,) + a.shape[1:], lambda b, h: (h, 0, 0))
    return pl.pallas_call(
        _mla_kernel,
        grid=(B, MLA_HEADS),
        in_specs=[seq(R), seq(ckv.shape[2]), seq(2 * MLA_ROPE), _resident(p['rope'].shape),
                  per_head(p['wq_mla']), per_head(p['wkv_mla'])],
        out_specs=pl.BlockSpec((1, L, MLA_V), lambda b, h: (b, 0, h)),
        out_shape=jax.ShapeDtypeStruct((B, L, MLA_HEADS * MLA_V), BF16),
        scratch_shapes=[pltpu.VMEM((L, dqk), BF16), pltpu.VMEM((dqk, L), BF16),
                        pltpu.VMEM((L, MXU_WIDTH_V7X), BF16)],
        compiler_params=_params(),
        name="mla",
    )(cq, ckv, kpe, p['rope'], p['wq_mla'], p['wkv_mla'])


def _out_kernel(x_ref, oa_ref, ob_ref, ga_ref, gb_ref, woa_ref, wob_ref, wout_ref,
                wgate_ref, wup_ref, wdown_ref, npost_ref, nfpre_ref, nfpost_ref, y_ref):
    tm = x_ref.shape[1]
    parts = [pl.ds(i * (tm // OUT_PARTS), tm // OUT_PARTS) for i in range(OUT_PARTS)]
    ya = [_dot(oa_ref[0, r, :], woa_ref[...]) for r in parts]
    yb = [_dot(ob_ref[0, r, :], wob_ref[...]) for r in parts]
    merged = [(ga_ref[0, r, :].astype(F32) * a + gb_ref[0, r, :].astype(F32) * b).astype(BF16)
              for r, a, b in zip(parts, ya, yb)]
    mo = [_dot(m, wout_ref[...]) for m in merged]
    x1 = [x_ref[0, r, :] + _rms(o, npost_ref[...]) for r, o in zip(parts, mo)]
    h = [_rms(x, nfpre_ref[...]).astype(BF16) for x in x1]
    gate = [_dot(v, wgate_ref[...]) for v in h]
    up = [_dot(v, wup_ref[...]) for v in h]
    act = [(_silu(g) * u).astype(BF16) for g, u in zip(gate, up)]
    down = [_dot(a, wdown_ref[...]) for a in act]
    for r, x, d in zip(parts, x1, down):
        y_ref[0, r, :] = x + _rms(d, nfpost_ref[...])


def _out(x, oa, ob, ga, gb, p):
    B, L, D = x.shape
    tm = min(OUT_TOKENS, L)
    tok = pl.BlockSpec((1, tm, D), lambda b, j: (b, j, 0))
    consts = [p['w_o_gla'], p['w_o_mla'], p['w_out'], p['w_gate'], p['w_up'], p['w_down'],
              p['g_post'], p['g_ffn_pre'], p['g_ffn_post']]
    return pl.pallas_call(
        _out_kernel,
        grid=(B, L // tm),
        in_specs=[tok] * 5 + [_resident(c.shape) for c in consts],
        out_specs=tok,
        out_shape=jax.ShapeDtypeStruct((B, L, D), F32),
        compiler_params=_params(),
        name="merge_ffn",
    )(x, oa, ob, ga, gb, *consts)


def _prepare(L, norm_mix_pre, w_in, gla_wa_fwd, gla_ba_fwd, gla_wa_bwd, gla_ba_bwd, gla_norm,
             w_o_gla, mla_norm_q, w_uq, mla_norm_kv, w_uk, w_uv, w_o_mla, w_out,
             norm_mix_post, norm_ffn_pre, w_gate, w_up, w_down, norm_ffn_post):
    D = w_in.shape[0]
    qk = GLA_HEADS * GLA_DK
    vw = GLA_HEADS * GLA_DV
    rq, rkv = mla_norm_q.shape[0], mla_norm_kv.shape[0]
    widths = (qk, qk, vw, vw, GLA_RANK, GLA_RANK, rq, rkv, MLA_ROPE, D, D)
    offs = np.concatenate([[0], np.cumsum(widths)]).tolist()
    piece = lambda i, j=None: w_in[:, offs[i]:offs[(i if j is None else j) + 1]]
    swap = np.concatenate([np.arange(MLA_ROPE // 2, MLA_ROPE), np.arange(MLA_ROPE // 2)])
    w_kr = piece(8)
    row = lambda a: a.reshape(1, -1).astype(F32)

    inv = ROPE_BASE ** (-jnp.arange(0, MLA_ROPE, 2, dtype=F32) / MLA_ROPE)
    ang = jnp.arange(L, dtype=F32)[:, None] * inv[None, :]
    cos, sin = jnp.cos(ang), jnp.sin(ang)

    uq = w_uq.reshape(rq, MLA_HEADS, MLA_NOPE + MLA_ROPE)
    wq_pe = uq[:, :, MLA_NOPE:]
    wq_mla = jnp.concatenate([uq[:, :, :MLA_NOPE], wq_pe, wq_pe[:, :, swap]], axis=2)
    wkv_mla = jnp.concatenate([w_uk.reshape(rkv, MLA_HEADS, MLA_NOPE),
                               w_uv.reshape(rkv, MLA_HEADS, MLA_V)], axis=2)
    zeros = jnp.zeros((GLA_RANK, qk), F32)
    return dict(
        g_pre=row(norm_mix_pre),
        wq=piece(0).astype(BF16), wk=piece(1).astype(BF16), wv=piece(2).astype(BF16),
        wg=piece(3).astype(BF16), wlat=piece(6, 7).astype(BF16),
        wmisc=jnp.concatenate([w_kr, w_kr[:, swap], piece(4, 5)], axis=1).astype(BF16),
        wga=piece(9).astype(BF16), wgb=piece(10).astype(BF16),
        nq=row(mla_norm_q), nkv=row(mla_norm_kv),
        rope=jnp.concatenate([cos, cos, -sin, sin], axis=1),
        wa=jnp.concatenate([jnp.concatenate([gla_wa_fwd, zeros], axis=1),
                            jnp.concatenate([zeros, gla_wa_bwd], axis=1)], axis=0).astype(BF16),
        ba=jnp.concatenate([row(gla_ba_fwd), row(gla_ba_bwd)], axis=1), gla_norm=row(gla_norm),
        wq_mla=jnp.transpose(wq_mla, (1, 0, 2)).astype(BF16),
        wkv_mla=jnp.transpose(wkv_mla, (1, 0, 2)).astype(BF16),
        w_o_gla=w_o_gla.astype(BF16), w_o_mla=w_o_mla.astype(BF16), w_out=w_out.astype(BF16),
        w_gate=w_gate.astype(BF16), w_up=w_up.astype(BF16), w_down=w_down.astype(BF16),
        g_post=row(norm_mix_post), g_ffn_pre=row(norm_ffn_pre), g_ffn_post=row(norm_ffn_post),
    )


def _layer(x, p):
    q, k, v, g, la_f, la_b, cq, ckv, kpe, ga, gb = _in_proj(x, p)
    oa = _gla(q, k, v, g, la_f, la_b, p)
    ob = _mla(cq, ckv, kpe, p)
    return _out(x, oa, ob, ga, gb, p)


def kernel(x_prompt, x_sample, norm_mix_pre, w_in, gla_wa_fwd, gla_ba_fwd, gla_wa_bwd, gla_ba_bwd, gla_norm, w_o_gla, mla_norm_q, w_uq, mla_norm_kv, w_uk, w_uv, w_o_mla, w_out, norm_mix_post, norm_ffn_pre, w_gate, w_up, w_down, norm_ffn_post):
    assert x_prompt.shape[1] == x_sample.shape[1]
    stacked = (norm_mix_pre, w_in, gla_wa_fwd, gla_ba_fwd, gla_wa_bwd, gla_ba_bwd, gla_norm,
               w_o_gla, mla_norm_q, w_uq, mla_norm_kv, w_uk, w_uv, w_o_mla, w_out,
               norm_mix_post, norm_ffn_pre, w_gate, w_up, w_down, norm_ffn_post)
    y_prompt, y_sample = x_prompt, x_sample
    for layer in range(norm_mix_pre.shape[0]):
        p = _prepare(x_prompt.shape[1], *[a[layer] for a in stacked])
        y_prompt, y_sample = _layer(y_prompt, p), _layer(y_sample, p)
    return (y_prompt, y_sample)
```

```python
import jax
import jax.numpy as jnp
import numpy as np
from jax import lax
from jax.experimental import pallas as pl
from jax.experimental.pallas import tpu as pltpu

F32 = jnp.float32
BF16 = jnp.bfloat16

EPS = 1e-6
GLA_HEADS = 4
GLA_DK = 128
GLA_DV = 256
GLA_RANK = 16
GLA_TEMP = 16.0
GLA_CHUNK = 128
MLA_HEADS = 8
MLA_NOPE = 128
MLA_ROPE = 64
MLA_V = 128
ROPE_BASE = 10000.0

IN_TOKENS = 512
IN_PARTS = 2
OUT_TOKENS = 512
OUT_PARTS = 2
Q_TOKENS = 256
Q_AHEAD = 1
MLA_HEADS_PER_STEP = 2
MXU_WIDTH_V7X = 256
LOG2_E = 1.4426950408889634
GLA_UNROLL = 16
INTRA_GROUP = 16
NORM_ROWS = 256
VMEM_LIMIT_BYTES = 56 * 1024 * 1024


def _rms(xf, g):
    ms = jnp.mean(xf * xf, axis=-1, keepdims=True)
    return xf * lax.rsqrt(ms + EPS) * g


def _silu(x):
    h = 0.5 * x
    return h + h * jnp.tanh(h)


def _dot(a, b):
    return jnp.dot(a, b, preferred_element_type=F32)


def _dot_nt(a, b):
    return lax.dot_general(a, b, (((1,), (1,)), ((), ())), preferred_element_type=F32)


def _rope(x2, table):
    y = x2 * table
    return (y + pltpu.roll(y, MLA_ROPE, axis=1))[:, :MLA_ROPE]


def _params():
    return pltpu.CompilerParams(
        dimension_semantics=("arbitrary", "arbitrary"),
        vmem_limit_bytes=VMEM_LIMIT_BYTES,
    )


def _resident(shape):
    nd = len(shape)
    return pl.BlockSpec(shape, lambda *_: (0,) * nd, pipeline_mode=pl.Buffered(1))


def _in_proj_kernel(x_ref, g_ref, rope_ref, nq_ref, nkv_ref,
                    wq_ref, wk_ref, wv_ref, wg_ref, wlat_ref, wmisc_ref, wga_ref, wgb_ref,
                    wa_ref, ba_ref,
                    q_out, k_out, v_out, g_out, laf_out, lab_out, cq_out, ckv_out, kpe_out,
                    ga_out, gb_out):
    tm = x_ref.shape[1]
    parts = [pl.ds(i * (tm // IN_PARTS), tm // IN_PARTS) for i in range(IN_PARTS)]
    h = [_rms(x_ref[0, r, :], g_ref[...]).astype(BF16) for r in parts]

    def project(w_ref, out, epilogue=lambda y: y):
        for r, y in zip(parts, [_dot(v, w_ref[...]) for v in h]):
            out[0, r, :] = epilogue(y).astype(out.dtype)

    misc = [_dot(v, wmisc_ref[...]) for v in h]
    for r, m in zip(parts, misc):
        kpe_out[0, r, :] = _rope(m[:, :2 * MLA_ROPE], rope_ref[r, :]).astype(BF16)

    lat_cols = slice(2 * MLA_ROPE, 2 * MLA_ROPE + 2 * GLA_RANK)
    zs = [_dot(m[:, lat_cols].astype(BF16), wa_ref[...]) + ba_ref[...] for m in misc]
    qk = GLA_HEADS * GLA_DK
    for r, z in zip(parts, zs):
        la = (jnp.minimum(z, 0.0) - jnp.log(1.0 + jnp.exp(-jnp.abs(z)))) * (LOG2_E / GLA_TEMP)
        laf_out[0, r, :] = la[:, :qk]
        lab_out[0, r, :] = la[:, qk:]

    project(wv_ref, v_out)
    project(wg_ref, g_out)
    project(wga_ref, ga_out, jax.nn.sigmoid)
    project(wq_ref, q_out)
    project(wgb_ref, gb_out, jax.nn.sigmoid)
    rq = nq_ref.shape[1]
    for r, lat in zip(parts, [_dot(v, wlat_ref[...]) for v in h]):
        cq_out[0, r, :] = _rms(lat[:, :rq], nq_ref[...]).astype(BF16)
        ckv_out[0, r, :] = _rms(lat[:, rq:], nkv_ref[...]).astype(BF16)
    project(wk_ref, k_out)


def _in_proj(x, p):
    B, L, D = x.shape
    tm = min(IN_TOKENS, L)
    grid = (B, L // tm)
    tok = lambda n, dt=BF16: (pl.BlockSpec((1, tm, n), lambda b, j: (b, j, 0)),
                              jax.ShapeDtypeStruct((B, L, n), dt))
    outs = [tok(p['wq'].shape[1]), tok(p['wk'].shape[1]), tok(p['wv'].shape[1]),
            tok(p['wg'].shape[1]), tok(p['wq'].shape[1], F32), tok(p['wq'].shape[1], F32),
            tok(p['nq'].shape[1]), tok(p['nkv'].shape[1]), tok(MLA_ROPE), tok(D), tok(D)]
    weights = [p['wq'], p['wk'], p['wv'], p['wg'], p['wlat'], p['wmisc'], p['wga'], p['wgb'],
               p['wa'], p['ba']]
    return pl.pallas_call(
        _in_proj_kernel,
        grid=grid,
        in_specs=[pl.BlockSpec((1, tm, D), lambda b, j: (b, j, 0)),
                  _resident(p['g_pre'].shape),
                  pl.BlockSpec((tm, 2 * MLA_ROPE), lambda b, j: (j, 0)),
                  _resident(p['nq'].shape), _resident(p['nkv'].shape)]
                 + [_resident(w.shape) for w in weights],
        out_specs=[o[0] for o in outs],
        out_shape=[o[1] for o in outs],
        compiler_params=_params(),
        name="in_proj",
    )(x, p['g_pre'], p['rope'], p['nq'], p['nkv'], *weights)


def _gla_kernel(q_ref, k_ref, v_ref, g_ref, laf_ref, lab_ref,
                gn_ref, o_ref, mf_ref, mb_ref, sf_ref, sb_ref, bf_ref, bb_ref,
                qif_ref, qib_ref, uf_ref, ub_ref, dtf_ref, dtb_ref, *rest):
    L = q_ref.shape[1]
    C = GLA_CHUNK
    nc = L // C
    nr = min(NORM_ROWS, L)
    mid = C // 2 - 1
    scale = GLA_DK ** -0.5
    a_refs, oacc_ref = rest[:-1], rest[-1]
    group = len(a_refs)
    assert nc % group == 0

    row = lax.broadcasted_iota(jnp.int32, (C, C), 0)
    col = lax.broadcasted_iota(jnp.int32, (C, C), 1)
    mf_ref[...] = jnp.where(col <= row, 1.0, 0.0).astype(BF16)
    mb_ref[...] = jnp.where(col >= row, 1.0, 0.0).astype(BF16)

    def cumsum(c, carry):
        rows = pl.ds(pl.multiple_of(c * C, C), C)
        for m_ref, la_ref, b_ref in ((mf_ref, laf_ref, bf_ref), (mb_ref, lab_ref, bb_ref)):
            la = la_ref[0, rows, :]
            hi = la.astype(BF16)
            lo = (la - hi.astype(F32)).astype(BF16)
            cs = _dot(m_ref[...], jnp.concatenate([hi, lo], axis=1))
            b_ref[rows, :] = cs[:, :GLA_DK] + cs[:, GLA_DK:]
        return carry

    lax.fori_loop(0, nc, cumsum, 0, unroll=GLA_UNROLL)

    dirs = ((bf_ref, qif_ref, uf_ref, dtf_ref, C - 1, mid, True),
            (bb_ref, qib_ref, ub_ref, dtb_ref, 0, mid + 1, False))

    def chunk_rows(c):
        return pl.ds(c * C if isinstance(c, int) else pl.multiple_of(c * C, C), C)

    def intra_out(grp):
        for slot, a_ref in enumerate(a_refs):
            rows = chunk_rows(group * grp + slot)
            oacc_ref[rows, :] = _dot(a_ref[chunk_rows(grp), :], v_ref[0, rows, :])

    lagged = nc > group
    if lagged:
        for a_ref in a_refs:
            a_ref[:C, :] = jnp.zeros((C, C), BF16)

    def intra(grp, carry):
        if lagged:
            intra_out(jnp.maximum(grp - 1, 0))
        for slot, a_ref in enumerate(a_refs):
            rows = chunk_rows(group * grp + slot)
            q = q_ref[0, rows, :].astype(F32) * scale
            k = k_ref[0, rows, :].astype(F32)
            v = v_ref[0, rows, :]
            a_sum = None
            for b_ref, qi_ref, u_ref, dt_ref, tot_row, ref_row, fwd in dirs:
                b_incl = b_ref[rows, :]
                b_mid = b_incl[ref_row:ref_row + 1, :]
                b_tot = b_incl[tot_row:tot_row + 1, :]
                b_rel = b_incl - b_mid
                q_rel = q * jnp.exp2(b_rel)
                k_rel = k * jnp.exp2(-b_rel)
                qi_ref[rows, :] = (q_rel * jnp.exp2(b_mid)).astype(BF16)
                k_end = k_rel * jnp.exp2(b_tot - b_mid)
                mask = (row >= col) if fwd else (row <= col)
                a = jnp.where(mask, _dot_nt(q_rel.astype(BF16), k_rel.astype(BF16)), 0.0)
                a_sum = a if a_sum is None else a_sum + a
                u_ref[rows, :] = _dot(k_end.T.astype(BF16), v)
                dt_ref[rows, :] = jnp.broadcast_to(jnp.exp2(b_tot), (C, GLA_DK)).T
            a_ref[chunk_rows(grp), :] = a_sum.astype(BF16)
        return carry

    if lagged:
        lax.fori_loop(0, nc // group, intra, 0)
    else:
        intra(0, 0)
    intra_out(nc // group - 1)

    sf_ref[...] = jnp.zeros_like(sf_ref)
    sb_ref[...] = jnp.zeros_like(sb_ref)

    def scan(i, carry):
        for s_ref, qi_ref, u_ref, dt_ref, c in ((sf_ref, qif_ref, uf_ref, dtf_ref, i),
                                                (sb_ref, qib_ref, ub_ref, dtb_ref, nc - 1 - i)):
            rows = pl.ds(pl.multiple_of(c * C, C), C)
            s = s_ref[...]
            oacc_ref[rows, :] += _dot(qi_ref[rows, :], s.astype(BF16))
            dt = dt_ref[rows, :]
            s_ref[...] = s * jnp.concatenate([dt] * (GLA_DV // C), axis=1) + u_ref[rows, :]
        return carry

    lax.fori_loop(0, nc, scan, 0, unroll=GLA_UNROLL)

    for r in range(L // nr):
        rows = pl.ds(r * nr, nr)
        y = _rms(oacc_ref[rows, :], gn_ref[...])
        g = g_ref[0, rows, :].astype(F32)
        o_ref[0, rows, :] = (y * _silu(g)).astype(BF16)


def _gla(q, k, v, g, la_f, la_b, p):
    B, L, _ = q.shape
    C = GLA_CHUNK
    group = min(INTRA_GROUP, L // C)
    head = lambda n: pl.BlockSpec((1, L, n), lambda b, h: (b, 0, h))
    return pl.pallas_call(
        _gla_kernel,
        grid=(B, GLA_HEADS),
        in_specs=[head(GLA_DK), head(GLA_DK), head(GLA_DV), head(GLA_DV),
                  head(GLA_DK), head(GLA_DK),
                  pl.BlockSpec((1, GLA_DV), lambda b, h: (0, 0))],
        out_specs=head(GLA_DV),
        out_shape=jax.ShapeDtypeStruct((B, L, GLA_HEADS * GLA_DV), BF16),
        scratch_shapes=[pltpu.VMEM((C, C), BF16), pltpu.VMEM((C, C), BF16),
                        pltpu.VMEM((GLA_DK, GLA_DV), F32), pltpu.VMEM((GLA_DK, GLA_DV), F32),
                        pltpu.VMEM((L, GLA_DK), F32), pltpu.VMEM((L, GLA_DK), F32),
                        pltpu.VMEM((L, GLA_DK), BF16), pltpu.VMEM((L, GLA_DK), BF16),
                        pltpu.VMEM((L, GLA_DV), F32), pltpu.VMEM((L, GLA_DV), F32),
                        pltpu.VMEM((L, GLA_DK), F32), pltpu.VMEM((L, GLA_DK), F32),
                        *[pltpu.VMEM((L // group, C), BF16) for _ in range(group)],
                        pltpu.VMEM((L, GLA_DV), F32)],
        compiler_params=_params(),
        name="gla",
    )(q, k, v, g, la_f, la_b, p['gla_norm'])


def _mla_kernel(cq_ref, ckv_ref, kpe_ref, rope_ref, wq_ref, wkv_ref, o_ref, *scratch):
    L = cq_ref.shape[1]
    scale = (MLA_NOPE + MLA_ROPE) ** -0.5 * LOG2_E
    tq = min(Q_TOKENS, L)
    nq = L // tq

    for hd in range(MLA_HEADS_PER_STEP):
        qs_ref, ks_ref, vs_ref = scratch[3 * hd:3 * hd + 3]
        qall = _dot(cq_ref[0], wq_ref[hd])
        qs_ref[:, :MLA_NOPE] = (qall[:, :MLA_NOPE] * scale).astype(BF16)
        qs_ref[:, MLA_NOPE:] = (_rope(qall[:, MLA_NOPE:], rope_ref[...]) * scale).astype(BF16)
        kv = _dot(ckv_ref[0], wkv_ref[hd])
        ks_ref[:, :MLA_NOPE] = kv[:, :MLA_NOPE].astype(BF16)
        ks_ref[:, MLA_NOPE:] = kpe_ref[0]
        vs_ref[:, :MLA_V] = kv[:, MLA_NOPE:].astype(BF16)
        vs_ref[:, MLA_V:] = jnp.ones((L, MXU_WIDTH_V7X - MLA_V), BF16)

        scores = lambda i: _dot_nt(qs_ref[i * tq:(i + 1) * tq, :], ks_ref[...])
        pending = [scores(j) for j in range(min(Q_AHEAD, nq))]
        for i in range(nq):
            if i + Q_AHEAD < nq:
                pending.append(scores(i + Q_AHEAD))
            s = pending.pop(0)
            e = jnp.exp2(s - jnp.max(s, axis=-1, keepdims=True)).astype(BF16)
            o_ext = _dot(e, vs_ref[...])
            o = o_ext[:, :MLA_V] * (1.0 / o_ext[:, MLA_V:MLA_V + 1])
            o_ref[0, i * tq:(i + 1) * tq, hd * MLA_V:(hd + 1) * MLA_V] = o.astype(BF16)


def _mla(cq, ckv, kpe, p):
    B, L, R = cq.shape
    dqk = MLA_NOPE + MLA_ROPE
    hps = MLA_HEADS_PER_STEP
    seq = lambda n: pl.BlockSpec((1, L, n), lambda b, h: (b, 0, 0))
    per_step = lambda a: pl.BlockSpec((hps,) + a.shape[1:], lambda b, h: (h, 0, 0))
    return pl.pallas_call(
        _mla_kernel,
        grid=(B, MLA_HEADS // hps),
        in_specs=[seq(R), seq(ckv.shape[2]), seq(MLA_ROPE), _resident(p['rope'].shape),
                  per_step(p['wq_mla']), per_step(p['wkv_mla'])],
        out_specs=pl.BlockSpec((1, L, hps * MLA_V), lambda b, h: (b, 0, h)),
        out_shape=jax.ShapeDtypeStruct((B, L, MLA_HEADS * MLA_V), BF16),
        scratch_shapes=[pltpu.VMEM((L, dqk), BF16), pltpu.VMEM((L, dqk), BF16),
                        pltpu.VMEM((L, MXU_WIDTH_V7X), BF16)] * hps,
        compiler_params=_params(),
        name="mla",
    )(cq, ckv, kpe, p['rope'], p['wq_mla'], p['wkv_mla'])


def _out_kernel(x_ref, oa_ref, ob_ref, ga_ref, gb_ref, woa_ref, wob_ref, wout_ref,
                wgate_ref, wup_ref, wdown_ref, npost_ref, nfpre_ref, nfpost_ref, y_ref):
    tm = x_ref.shape[1]
    parts = [pl.ds(i * (tm // OUT_PARTS), tm // OUT_PARTS) for i in range(OUT_PARTS)]
    ya = [_dot(oa_ref[0, r, :], woa_ref[...]) for r in parts]
    yb = [_dot(ob_ref[0, r, :], wob_ref[...]) for r in parts]
    merged = [(ga_ref[0, r, :].astype(F32) * a + gb_ref[0, r, :].astype(F32) * b).astype(BF16)
              for r, a, b in zip(parts, ya, yb)]
    mo = [_dot(m, wout_ref[...]) for m in merged]
    x1 = [x_ref[0, r, :] + _rms(o, npost_ref[...]) for r, o in zip(parts, mo)]
    h = [_rms(x, nfpre_ref[...]).astype(BF16) for x in x1]
    gate = [_dot(v, wgate_ref[...]) for v in h]
    up = [_dot(v, wup_ref[...]) for v in h]
    act = [(_silu(g) * u).astype(BF16) for g, u in zip(gate, up)]
    down = [_dot(a, wdown_ref[...]) for a in act]
    for r, x, d in zip(parts, x1, down):
        y_ref[0, r, :] = x + _rms(d, nfpost_ref[...])


def _out(x, oa, ob, ga, gb, p):
    B, L, D = x.shape
    tm = min(OUT_TOKENS, L)
    tok = pl.BlockSpec((1, tm, D), lambda b, j: (b, j, 0))
    consts = [p['w_o_gla'], p['w_o_mla'], p['w_out'], p['w_gate'], p['w_up'], p['w_down'],
              p['g_post'], p['g_ffn_pre'], p['g_ffn_post']]
    return pl.pallas_call(
        _out_kernel,
        grid=(B, L // tm),
        in_specs=[tok] * 5 + [_resident(c.shape) for c in consts],
        out_specs=tok,
        out_shape=jax.ShapeDtypeStruct((B, L, D), F32),
        compiler_params=_params(),
        name="merge_ffn",
    )(x, oa, ob, ga, gb, *consts)


def _prepare(L, norm_mix_pre, w_in, gla_wa_fwd, gla_ba_fwd, gla_wa_bwd, gla_ba_bwd, gla_norm,
             w_o_gla, mla_norm_q, w_uq, mla_norm_kv, w_uk, w_uv, w_o_mla, w_out,
             norm_mix_post, norm_ffn_pre, w_gate, w_up, w_down, norm_ffn_post):
    D = w_in.shape[0]
    qk = GLA_HEADS * GLA_DK
    vw = GLA_HEADS * GLA_DV
    rq, rkv = mla_norm_q.shape[0], mla_norm_kv.shape[0]
    widths = (qk, qk, vw, vw, GLA_RANK, GLA_RANK, rq, rkv, MLA_ROPE, D, D)
    offs = np.concatenate([[0], np.cumsum(widths)]).tolist()
    piece = lambda i, j=None: w_in[:, offs[i]:offs[(i if j is None else j) + 1]]
    swap = np.concatenate([np.arange(MLA_ROPE // 2, MLA_ROPE), np.arange(MLA_ROPE // 2)])
    w_kr = piece(8)
    row = lambda a: a.reshape(1, -1).astype(F32)

    inv = ROPE_BASE ** (-jnp.arange(0, MLA_ROPE, 2, dtype=F32) / MLA_ROPE)
    ang = jnp.arange(L, dtype=F32)[:, None] * inv[None, :]
    cos, sin = jnp.cos(ang), jnp.sin(ang)

    uq = w_uq.reshape(rq, MLA_HEADS, MLA_NOPE + MLA_ROPE)
    wq_pe = uq[:, :, MLA_NOPE:]
    wq_mla = jnp.concatenate([uq[:, :, :MLA_NOPE], wq_pe, wq_pe[:, :, swap]], axis=2)
    wkv_mla = jnp.concatenate([w_uk.reshape(rkv, MLA_HEADS, MLA_NOPE),
                               w_uv.reshape(rkv, MLA_HEADS, MLA_V)], axis=2)
    zeros = jnp.zeros((GLA_RANK, qk), F32)
    return dict(
        g_pre=row(norm_mix_pre),
        wq=piece(0).astype(BF16), wk=piece(1).astype(BF16), wv=piece(2).astype(BF16),
        wg=piece(3).astype(BF16), wlat=piece(6, 7).astype(BF16),
        wmisc=jnp.concatenate([w_kr, w_kr[:, swap], piece(4, 5)], axis=1).astype(BF16),
        wga=piece(9).astype(BF16), wgb=piece(10).astype(BF16),
        nq=row(mla_norm_q), nkv=row(mla_norm_kv),
        rope=jnp.concatenate([cos, cos, -sin, sin], axis=1),
        wa=jnp.concatenate([jnp.concatenate([gla_wa_fwd, zeros], axis=1),
                            jnp.concatenate([zeros, gla_wa_bwd], axis=1)], axis=0).astype(BF16),
        ba=jnp.concatenate([row(gla_ba_fwd), row(gla_ba_bwd)], axis=1), gla_norm=row(gla_norm),
        wq_mla=jnp.transpose(wq_mla, (1, 0, 2)).astype(BF16),
        wkv_mla=jnp.transpose(wkv_mla, (1, 0, 2)).astype(BF16),
        w_o_gla=w_o_gla.astype(BF16), w_o_mla=w_o_mla.astype(BF16), w_out=w_out.astype(BF16),
        w_gate=w_gate.astype(BF16), w_up=w_up.astype(BF16), w_down=w_down.astype(BF16),
        g_post=row(norm_mix_post), g_ffn_pre=row(norm_ffn_pre), g_ffn_post=row(norm_ffn_post),
    )


def _layer(x, p):
    q, k, v, g, la_f, la_b, cq, ckv, kpe, ga, gb = _in_proj(x, p)
    oa = _gla(q, k, v, g, la_f, la_b, p)
    ob = _mla(cq, ckv, kpe, p)
    return _out(x, oa, ob, ga, gb, p)


def kernel(x_prompt, x_sample, norm_mix_pre, w_in, gla_wa_fwd, gla_ba_fwd, gla_wa_bwd, gla_ba_bwd, gla_norm, w_o_gla, mla_norm_q, w_uq, mla_norm_kv, w_uk, w_uv, w_o_mla, w_out, norm_mix_post, norm_ffn_pre, w_gate, w_up, w_down, norm_ffn_post):
    assert x_prompt.shape[1] == x_sample.shape[1]
    stacked = (norm_mix_pre, w_in, gla_wa_fwd, gla_ba_fwd, gla_wa_bwd, gla_ba_bwd, gla_norm,
               w_o_gla, mla_norm_q, w_uq, mla_norm_kv, w_uk, w_uv, w_o_mla, w_out,
               norm_mix_post, norm_ffn_pre, w_gate, w_up, w_down, norm_ffn_post)
    y_prompt, y_sample = x_prompt, x_sample
    for layer in range(norm_mix_pre.shape[0]):
        p = _prepare(x_prompt.shape[1], *[a[layer] for a in stacked])
        y_prompt, y_sample = _layer(y_prompt, p), _layer(y_sample, p)
    return (y_prompt, y_sample)
```

```python
import jax
import jax.numpy as jnp
import numpy as np
from jax import lax
from jax.experimental import pallas as pl
from jax.experimental.pallas import tpu as pltpu

F32 = jnp.float32
BF16 = jnp.bfloat16

EPS = 1e-6
GLA_HEADS = 4
GLA_DK = 128
GLA_DV = 256
GLA_RANK = 16
GLA_TEMP = 16.0
GLA_CHUNK = 128
MLA_HEADS = 8
MLA_NOPE = 128
MLA_ROPE = 64
MLA_V = 128
ROPE_BASE = 10000.0

IN_TOKENS = 512
IN_PARTS = 2
OUT_TOKENS = 512
OUT_PARTS = 2
Q_TOKENS = 256
Q_AHEAD = 1
MLA_HEADS_PER_STEP = 2
MXU_WIDTH_V7X = 256
LOG2_E = 1.4426950408889634
GLA_UNROLL = 16
INTRA_GROUP = 16
GLA_HEADS_PER_STEP = 2
NORM_ROWS = 256
VMEM_LIMIT_BYTES = 56 * 1024 * 1024


def _rms(xf, g):
    ms = jnp.mean(xf * xf, axis=-1, keepdims=True)
    return xf * lax.rsqrt(ms + EPS) * g


def _silu(x):
    h = 0.5 * x
    return h + h * jnp.tanh(h)


def _dot(a, b):
    return jnp.dot(a, b, preferred_element_type=F32)


def _dot_nt(a, b):
    return lax.dot_general(a, b, (((1,), (1,)), ((), ())), preferred_element_type=F32)


def _rope(x2, table):
    y = x2 * table
    return (y + pltpu.roll(y, MLA_ROPE, axis=1))[:, :MLA_ROPE]


def _params():
    return pltpu.CompilerParams(
        dimension_semantics=("arbitrary", "arbitrary"),
        vmem_limit_bytes=VMEM_LIMIT_BYTES,
    )


def _resident(shape):
    nd = len(shape)
    return pl.BlockSpec(shape, lambda *_: (0,) * nd, pipeline_mode=pl.Buffered(1))


def _in_proj_kernel(x_ref, g_ref, rope_ref, nq_ref, nkv_ref,
                    wq_ref, wk_ref, wv_ref, wg_ref, wlat_ref, wmisc_ref, wga_ref, wgb_ref,
                    wa_ref, ba_ref,
                    q_out, k_out, v_out, g_out, laf_out, lab_out, cq_out, ckv_out, kpe_out,
                    ga_out, gb_out):
    tm = x_ref.shape[1]
    parts = [pl.ds(i * (tm // IN_PARTS), tm // IN_PARTS) for i in range(IN_PARTS)]
    h = [_rms(x_ref[0, r, :], g_ref[...]).astype(BF16) for r in parts]

    def project(w_ref, out, epilogue=lambda y: y):
        for r, y in zip(parts, [_dot(v, w_ref[...]) for v in h]):
            out[0, r, :] = epilogue(y).astype(out.dtype)

    misc = [_dot(v, wmisc_ref[...]) for v in h]
    for r, m in zip(parts, misc):
        kpe_out[0, r, :] = _rope(m[:, :2 * MLA_ROPE], rope_ref[r, :]).astype(BF16)

    lat_cols = slice(2 * MLA_ROPE, 2 * MLA_ROPE + 2 * GLA_RANK)
    zs = [_dot(m[:, lat_cols].astype(BF16), wa_ref[...]) + ba_ref[...] for m in misc]
    qk = GLA_HEADS * GLA_DK
    for r, z in zip(parts, zs):
        la = (jnp.minimum(z, 0.0) - jnp.log(1.0 + jnp.exp(-jnp.abs(z)))) * (LOG2_E / GLA_TEMP)
        laf_out[0, r, :] = la[:, :qk]
        lab_out[0, r, :] = la[:, qk:]

    project(wv_ref, v_out)
    project(wg_ref, g_out)
    project(wga_ref, ga_out, jax.nn.sigmoid)
    project(wq_ref, q_out)
    project(wgb_ref, gb_out, jax.nn.sigmoid)
    rq = nq_ref.shape[1]
    for r, lat in zip(parts, [_dot(v, wlat_ref[...]) for v in h]):
        cq_out[0, r, :] = _rms(lat[:, :rq], nq_ref[...]).astype(BF16)
        ckv_out[0, r, :] = _rms(lat[:, rq:], nkv_ref[...]).astype(BF16)
    project(wk_ref, k_out)


def _in_proj(x, p):
    B, L, D = x.shape
    tm = min(IN_TOKENS, L)
    grid = (B, L // tm)
    tok = lambda n, dt=BF16: (pl.BlockSpec((1, tm, n), lambda b, j: (b, j, 0)),
                              jax.ShapeDtypeStruct((B, L, n), dt))
    outs = [tok(p['wq'].shape[1]), tok(p['wk'].shape[1]), tok(p['wv'].shape[1]),
            tok(p['wg'].shape[1]), tok(p['wq'].shape[1], F32), tok(p['wq'].shape[1], F32),
            tok(p['nq'].shape[1]), tok(p['nkv'].shape[1]), tok(MLA_ROPE), tok(D), tok(D)]
    weights = [p['wq'], p['wk'], p['wv'], p['wg'], p['wlat'], p['wmisc'], p['wga'], p['wgb'],
               p['wa'], p['ba']]
    return pl.pallas_call(
        _in_proj_kernel,
        grid=grid,
        in_specs=[pl.BlockSpec((1, tm, D), lambda b, j: (b, j, 0)),
                  _resident(p['g_pre'].shape),
                  pl.BlockSpec((tm, 2 * MLA_ROPE), lambda b, j: (j, 0)),
                  _resident(p['nq'].shape), _resident(p['nkv'].shape)]
                 + [_resident(w.shape) for w in weights],
        out_specs=[o[0] for o in outs],
        out_shape=[o[1] for o in outs],
        compiler_params=_params(),
        name="in_proj",
    )(x, p['g_pre'], p['rope'], p['nq'], p['nkv'], *weights)


def _gla_kernel(q_ref, k_ref, v_ref, g_ref, laf_ref, lab_ref, gn_ref, o_ref, *scratch):
    for hd in range(GLA_HEADS_PER_STEP):
        kc = pl.ds(hd * GLA_DK, GLA_DK)
        vc = pl.ds(hd * GLA_DV, GLA_DV)
        _gla_head(q_ref.at[:, :, kc], k_ref.at[:, :, kc], v_ref.at[:, :, vc], g_ref.at[:, :, vc],
                  laf_ref.at[:, :, kc], lab_ref.at[:, :, kc], gn_ref, o_ref.at[:, :, vc], *scratch)


def _gla_head(q_ref, k_ref, v_ref, g_ref, laf_ref, lab_ref,
              gn_ref, o_ref, mf_ref, mb_ref, sf_ref, sb_ref, bf_ref, bb_ref,
              qif_ref, qib_ref, uf_ref, ub_ref, dtf_ref, dtb_ref, *rest):
    L = q_ref.shape[1]
    C = GLA_CHUNK
    nc = L // C
    nr = min(NORM_ROWS, L)
    mid = C // 2 - 1
    scale = GLA_DK ** -0.5
    a_refs, oacc_ref = rest[:-1], rest[-1]
    group = len(a_refs)
    assert nc % group == 0

    row = lax.broadcasted_iota(jnp.int32, (C, C), 0)
    col = lax.broadcasted_iota(jnp.int32, (C, C), 1)
    mf_ref[...] = jnp.where(col <= row, 1.0, 0.0).astype(BF16)
    mb_ref[...] = jnp.where(col >= row, 1.0, 0.0).astype(BF16)

    def cumsum(c, carry):
        rows = pl.ds(pl.multiple_of(c * C, C), C)
        for m_ref, la_ref, b_ref in ((mf_ref, laf_ref, bf_ref), (mb_ref, lab_ref, bb_ref)):
            la = la_ref[0, rows, :]
            hi = la.astype(BF16)
            lo = (la - hi.astype(F32)).astype(BF16)
            cs = _dot(m_ref[...], jnp.concatenate([hi, lo], axis=1))
            b_ref[rows, :] = cs[:, :GLA_DK] + cs[:, GLA_DK:]
        return carry

    lax.fori_loop(0, nc, cumsum, 0, unroll=GLA_UNROLL)

    dirs = ((bf_ref, qif_ref, uf_ref, dtf_ref, C - 1, mid, True),
            (bb_ref, qib_ref, ub_ref, dtb_ref, 0, mid + 1, False))

    def chunk_rows(c):
        return pl.ds(c * C if isinstance(c, int) else pl.multiple_of(c * C, C), C)

    def intra_out(grp):
        for slot, a_ref in enumerate(a_refs):
            rows = chunk_rows(group * grp + slot)
            oacc_ref[rows, :] = _dot(a_ref[chunk_rows(grp), :], v_ref[0, rows, :])

    lagged = nc > group
    if lagged:
        for a_ref in a_refs:
            a_ref[:C, :] = jnp.zeros((C, C), BF16)

    def intra(grp, carry):
        if lagged:
            intra_out(jnp.maximum(grp - 1, 0))
        for slot, a_ref in enumerate(a_refs):
            rows = chunk_rows(group * grp + slot)
            q = q_ref[0, rows, :].astype(F32) * scale
            k = k_ref[0, rows, :].astype(F32)
            v = v_ref[0, rows, :]
            a_sum = None
            for b_ref, qi_ref, u_ref, dt_ref, tot_row, ref_row, fwd in dirs:
                b_incl = b_ref[rows, :]
                b_mid = b_incl[ref_row:ref_row + 1, :]
                b_tot = b_incl[tot_row:tot_row + 1, :]
                b_rel = b_incl - b_mid
                q_rel = q * jnp.exp2(b_rel)
                k_rel = k * jnp.exp2(-b_rel)
                qi_ref[rows, :] = (q_rel * jnp.exp2(b_mid)).astype(BF16)
                k_end = k_rel * jnp.exp2(b_tot - b_mid)
                mask = (row >= col) if fwd else (row <= col)
                a = jnp.where(mask, _dot_nt(q_rel.astype(BF16), k_rel.astype(BF16)), 0.0)
                a_sum = a if a_sum is None else a_sum + a
                u_ref[rows, :] = _dot(k_end.T.astype(BF16), v)
                dt_ref[rows, :] = jnp.broadcast_to(jnp.exp2(b_tot), (C, GLA_DK)).T
            a_ref[chunk_rows(grp), :] = a_sum.astype(BF16)
        return carry

    if lagged:
        lax.fori_loop(0, nc // group, intra, 0)
    else:
        intra(0, 0)
    intra_out(nc // group - 1)

    sf_ref[...] = jnp.zeros_like(sf_ref)
    sb_ref[...] = jnp.zeros_like(sb_ref)

    def scan(i, carry):
        for s_ref, qi_ref, u_ref, dt_ref, c in ((sf_ref, qif_ref, uf_ref, dtf_ref, i),
                                                (sb_ref, qib_ref, ub_ref, dtb_ref, nc - 1 - i)):
            rows = pl.ds(pl.multiple_of(c * C, C), C)
            s = s_ref[...]
            oacc_ref[rows, :] += _dot(qi_ref[rows, :], s.astype(BF16))
            dt = dt_ref[rows, :]
            s_ref[...] = s * jnp.concatenate([dt] * (GLA_DV // C), axis=1) + u_ref[rows, :]
        return carry

    lax.fori_loop(0, nc, scan, 0, unroll=GLA_UNROLL)

    for r in range(L // nr):
        rows = pl.ds(r * nr, nr)
        y = _rms(oacc_ref[rows, :], gn_ref[...])
        g = g_ref[0, rows, :].astype(F32)
        o_ref[0, rows, :] = (y * _silu(g)).astype(BF16)


def _gla(q, k, v, g, la_f, la_b, p):
    B, L, _ = q.shape
    C = GLA_CHUNK
    group = min(INTRA_GROUP, L // C)
    hps = GLA_HEADS_PER_STEP
    head = lambda n: pl.BlockSpec((1, L, hps * n), lambda b, h: (b, 0, h))
    return pl.pallas_call(
        _gla_kernel,
        grid=(B, GLA_HEADS // hps),
        in_specs=[head(GLA_DK), head(GLA_DK), head(GLA_DV), head(GLA_DV),
                  head(GLA_DK), head(GLA_DK),
                  pl.BlockSpec((1, GLA_DV), lambda b, h: (0, 0))],
        out_specs=head(GLA_DV),
        out_shape=jax.ShapeDtypeStruct((B, L, GLA_HEADS * GLA_DV), BF16),
        scratch_shapes=[pltpu.VMEM((C, C), BF16), pltpu.VMEM((C, C), BF16),
                        pltpu.VMEM((GLA_DK, GLA_DV), F32), pltpu.VMEM((GLA_DK, GLA_DV), F32),
                        pltpu.VMEM((L, GLA_DK), F32), pltpu.VMEM((L, GLA_DK), F32),
                        pltpu.VMEM((L, GLA_DK), BF16), pltpu.VMEM((L, GLA_DK), BF16),
                        pltpu.VMEM((L, GLA_DV), F32), pltpu.VMEM((L, GLA_DV), F32),
                        pltpu.VMEM((L, GLA_DK), F32), pltpu.VMEM((L, GLA_DK), F32),
                        *[pltpu.VMEM((L // group, C), BF16) for _ in range(group)],
                        pltpu.VMEM((L, GLA_DV), F32)],
        compiler_params=_params(),
        name="gla",
    )(q, k, v, g, la_f, la_b, p['gla_norm'])


def _mla_kernel(cq_ref, ckv_ref, kpe_ref, rope_ref, wq_ref, wkv_ref, o_ref, *scratch):
    L = cq_ref.shape[1]
    scale = (MLA_NOPE + MLA_ROPE) ** -0.5 * LOG2_E
    tq = min(Q_TOKENS, L)
    nq = L // tq

    for hd in range(MLA_HEADS_PER_STEP):
        qs_ref, ks_ref, vs_ref = scratch[3 * hd:3 * hd + 3]
        qall = _dot(cq_ref[0], wq_ref[hd])
        qs_ref[:, :MLA_NOPE] = (qall[:, :MLA_NOPE] * scale).astype(BF16)
        qs_ref[:, MLA_NOPE:] = (_rope(qall[:, MLA_NOPE:], rope_ref[...]) * scale).astype(BF16)
        kv = _dot(ckv_ref[0], wkv_ref[hd])
        ks_ref[:, :MLA_NOPE] = kv[:, :MLA_NOPE].astype(BF16)
        ks_ref[:, MLA_NOPE:] = kpe_ref[0]
        vs_ref[:, :MLA_V] = kv[:, MLA_NOPE:].astype(BF16)
        vs_ref[:, MLA_V:] = jnp.ones((L, MXU_WIDTH_V7X - MLA_V), BF16)

        scores = lambda i: _dot_nt(qs_ref[i * tq:(i + 1) * tq, :], ks_ref[...])
        pending = [scores(j) for j in range(min(Q_AHEAD, nq))]
        for i in range(nq):
            if i + Q_AHEAD < nq:
                pending.append(scores(i + Q_AHEAD))
            s = pending.pop(0)
            e = jnp.exp2(s - jnp.max(s, axis=-1, keepdims=True)).astype(BF16)
            o_ext = _dot(e, vs_ref[...])
            o = o_ext[:, :MLA_V] * (1.0 / o_ext[:, MLA_V:MLA_V + 1])
            o_ref[0, i * tq:(i + 1) * tq, hd * MLA_V:(hd + 1) * MLA_V] = o.astype(BF16)


def _mla(cq, ckv, kpe, p):
    B, L, R = cq.shape
    dqk = MLA_NOPE + MLA_ROPE
    hps = MLA_HEADS_PER_STEP
    seq = lambda n: pl.BlockSpec((1, L, n), lambda b, h: (b, 0, 0))
    per_step = lambda a: pl.BlockSpec((hps,) + a.shape[1:], lambda b, h: (h, 0, 0))
    return pl.pallas_call(
        _mla_kernel,
        grid=(B, MLA_HEADS // hps),
        in_specs=[seq(R), seq(ckv.shape[2]), seq(MLA_ROPE), _resident(p['rope'].shape),
                  per_step(p['wq_mla']), per_step(p['wkv_mla'])],
        out_specs=pl.BlockSpec((1, L, hps * MLA_V), lambda b, h: (b, 0, h)),
        out_shape=jax.ShapeDtypeStruct((B, L, MLA_HEADS * MLA_V), BF16),
        scratch_shapes=[pltpu.VMEM((L, dqk), BF16), pltpu.VMEM((L, dqk), BF16),
                        pltpu.VMEM((L, MXU_WIDTH_V7X), BF16)] * hps,
        compiler_params=_params(),
        name="mla",
    )(cq, ckv, kpe, p['rope'], p['wq_mla'], p['wkv_mla'])


def _out_kernel(x_ref, oa_ref, ob_ref, ga_ref, gb_ref, woa_ref, wob_ref, wout_ref,
                wgate_ref, wup_ref, wdown_ref, npost_ref, nfpre_ref, nfpost_ref, y_ref):
    tm = x_ref.shape[1]
    parts = [pl.ds(i * (tm // OUT_PARTS), tm // OUT_PARTS) for i in range(OUT_PARTS)]
    ya = [_dot(oa_ref[0, r, :], woa_ref[...]) for r in parts]
    yb = [_dot(ob_ref[0, r, :], wob_ref[...]) for r in parts]
    merged = [(ga_ref[0, r, :].astype(F32) * a + gb_ref[0, r, :].astype(F32) * b).astype(BF16)
              for r, a, b in zip(parts, ya, yb)]
    mo = [_dot(m, wout_ref[...]) for m in merged]
    x1 = [x_ref[0, r, :] + _rms(o, npost_ref[...]) for r, o in zip(parts, mo)]
    h = [_rms(x, nfpre_ref[...]).astype(BF16) for x in x1]
    gate = [_dot(v, wgate_ref[...]) for v in h]
    up = [_dot(v, wup_ref[...]) for v in h]
    act = [(_silu(g) * u).astype(BF16) for g, u in zip(gate, up)]
    down = [_dot(a, wdown_ref[...]) for a in act]
    for r, x, d in zip(parts, x1, down):
        y_ref[0, r, :] = x + _rms(d, nfpost_ref[...])


def _out(x, oa, ob, ga, gb, p):
    B, L, D = x.shape
    tm = min(OUT_TOKENS, L)
    tok = pl.BlockSpec((1, tm, D), lambda b, j: (b, j, 0))
    consts = [p['w_o_gla'], p['w_o_mla'], p['w_out'], p['w_gate'], p['w_up'], p['w_down'],
              p['g_post'], p['g_ffn_pre'], p['g_ffn_post']]
    return pl.pallas_call(
        _out_kernel,
        grid=(B, L // tm),
        in_specs=[tok] * 5 + [_resident(c.shape) for c in consts],
        out_specs=tok,
        out_shape=jax.ShapeDtypeStruct((B, L, D), F32),
        compiler_params=_params(),
        name="merge_ffn",
    )(x, oa, ob, ga, gb, *consts)


def _prepare(L, norm_mix_pre, w_in, gla_wa_fwd, gla_ba_fwd, gla_wa_bwd, gla_ba_bwd, gla_norm,
             w_o_gla, mla_norm_q, w_uq, mla_norm_kv, w_uk, w_uv, w_o_mla, w_out,
             norm_mix_post, norm_ffn_pre, w_gate, w_up, w_down, norm_ffn_post):
    D = w_in.shape[0]
    qk = GLA_HEADS * GLA_DK
    vw = GLA_HEADS * GLA_DV
    rq, rkv = mla_norm_q.shape[0], mla_norm_kv.shape[0]
    widths = (qk, qk, vw, vw, GLA_RANK, GLA_RANK, rq, rkv, MLA_ROPE, D, D)
    offs = np.concatenate([[0], np.cumsum(widths)]).tolist()
    piece = lambda i, j=None: w_in[:, offs[i]:offs[(i if j is None else j) + 1]]
    swap = np.concatenate([np.arange(MLA_ROPE // 2, MLA_ROPE), np.arange(MLA_ROPE // 2)])
    w_kr = piece(8)
    row = lambda a: a.reshape(1, -1).astype(F32)

    inv = ROPE_BASE ** (-jnp.arange(0, MLA_ROPE, 2, dtype=F32) / MLA_ROPE)
    ang = jnp.arange(L, dtype=F32)[:, None] * inv[None, :]
    cos, sin = jnp.cos(ang), jnp.sin(ang)

    uq = w_uq.reshape(rq, MLA_HEADS, MLA_NOPE + MLA_ROPE)
    wq_pe = uq[:, :, MLA_NOPE:]
    wq_mla = jnp.concatenate([uq[:, :, :MLA_NOPE], wq_pe, wq_pe[:, :, swap]], axis=2)
    wkv_mla = jnp.concatenate([w_uk.reshape(rkv, MLA_HEADS, MLA_NOPE),
                               w_uv.reshape(rkv, MLA_HEADS, MLA_V)], axis=2)
    zeros = jnp.zeros((GLA_RANK, qk), F32)
    return dict(
        g_pre=row(norm_mix_pre),
        wq=piece(0).astype(BF16), wk=piece(1).astype(BF16), wv=piece(2).astype(BF16),
        wg=piece(3).astype(BF16), wlat=piece(6, 7).astype(BF16),
        wmisc=jnp.concatenate([w_kr, w_kr[:, swap], piece(4, 5)], axis=1).astype(BF16),
        wga=piece(9).astype(BF16), wgb=piece(10).astype(BF16),
        nq=row(mla_norm_q), nkv=row(mla_norm_kv),
        rope=jnp.concatenate([cos, cos, -sin, sin], axis=1),
        wa=jnp.concatenate([jnp.concatenate([gla_wa_fwd, zeros], axis=1),
                            jnp.concatenate([zeros, gla_wa_bwd], axis=1)], axis=0).astype(BF16),
        ba=jnp.concatenate([row(gla_ba_fwd), row(gla_ba_bwd)], axis=1), gla_norm=row(gla_norm),
        wq_mla=jnp.transpose(wq_mla, (1, 0, 2)).astype(BF16),
        wkv_mla=jnp.transpose(wkv_mla, (1, 0, 2)).astype(BF16),
        w_o_gla=w_o_gla.astype(BF16), w_o_mla=w_o_mla.astype(BF16), w_out=w_out.astype(BF16),
        w_gate=w_gate.astype(BF16), w_up=w_up.astype(BF16), w_down=w_down.astype(BF16),
        g_post=row(norm_mix_post), g_ffn_pre=row(norm_ffn_pre), g_ffn_post=row(norm_ffn_post),
    )


def _layer(x, p):
    q, k, v, g, la_f, la_b, cq, ckv, kpe, ga, gb = _in_proj(x, p)
    oa = _gla(q, k, v, g, la_f, la_b, p)
    ob = _mla(cq, ckv, kpe, p)
    return _out(x, oa, ob, ga, gb, p)


def kernel(x_prompt, x_sample, norm_mix_pre, w_in, gla_wa_fwd, gla_ba_fwd, gla_wa_bwd, gla_ba_bwd, gla_norm, w_o_gla, mla_norm_q, w_uq, mla_norm_kv, w_uk, w_uv, w_o_mla, w_out, norm_mix_post, norm_ffn_pre, w_gate, w_up, w_down, norm_ffn_post):
    assert x_prompt.shape[1] == x_sample.shape[1]
    stacked = (norm_mix_pre, w_in, gla_wa_fwd, gla_ba_fwd, gla_wa_bwd, gla_ba_bwd, gla_norm,
               w_o_gla, mla_norm_q, w_uq, mla_norm_kv, w_uk, w_uv, w_o_mla, w_out,
               norm_mix_post, norm_ffn_pre, w_gate, w_up, w_down, norm_ffn_post)
    y_prompt, y_sample = x_prompt, x_sample
    for layer in range(norm_mix_pre.shape[0]):
        p = _prepare(x_prompt.shape[1], *[a[layer] for a in stacked])
        y_prompt, y_sample = _layer(y_prompt, p), _layer(y_sample, p)
    return (y_prompt, y_sample)
```

```python
import jax
import jax.numpy as jnp
import numpy as np
from jax import lax
from jax.experimental import pallas as pl
from jax.experimental.pallas import tpu as pltpu

F32 = jnp.float32
BF16 = jnp.bfloat16

EPS = 1e-6
GLA_HEADS = 4
GLA_DK = 128
GLA_DV = 256
GLA_RANK = 16
GLA_TEMP = 16.0
GLA_CHUNK = 128
MLA_HEADS = 8
MLA_NOPE = 128
MLA_ROPE = 64
MLA_V = 128
ROPE_BASE = 10000.0

IN_TOKENS = 512
IN_PARTS = 2
OUT_TOKENS = 512
OUT_PARTS = 2
Q_TOKENS = 256
Q_AHEAD = 1
MLA_HEADS_PER_STEP = 2
MXU_WIDTH_V7X = 256
LOG2_E = 1.4426950408889634
GLA_UNROLL = 16
INTRA_GROUP = 16
GLA_HEADS_PER_STEP = 2
NORM_ROWS = 256
VMEM_LIMIT_BYTES = 56 * 1024 * 1024


def _rms(xf, g):
    ms = jnp.mean(xf * xf, axis=-1, keepdims=True)
    return xf * lax.rsqrt(ms + EPS) * g


def _silu(x):
    h = 0.5 * x
    return h + h * jnp.tanh(h)


def _dot(a, b):
    return jnp.dot(a, b, preferred_element_type=F32)


def _dot_nt(a, b):
    return lax.dot_general(a, b, (((1,), (1,)), ((), ())), preferred_element_type=F32)


def _rope(x2, table):
    y = x2 * table
    return (y + pltpu.roll(y, MLA_ROPE, axis=1))[:, :MLA_ROPE]


def _params():
    return pltpu.CompilerParams(
        dimension_semantics=("arbitrary", "arbitrary"),
        vmem_limit_bytes=VMEM_LIMIT_BYTES,
    )


def _resident(shape):
    nd = len(shape)
    return pl.BlockSpec(shape, lambda *_: (0,) * nd, pipeline_mode=pl.Buffered(1))


def _in_proj_kernel(x_ref, g_ref, rope_ref, nq_ref, nkv_ref,
                    wq_ref, wk_ref, wv_ref, wg_ref, wlat_ref, wmisc_ref, wga_ref, wgb_ref,
                    wa_ref, ba_ref,
                    q_out, k_out, v_out, g_out, laf_out, lab_out, cq_out, ckv_out, kpe_out,
                    ga_out, gb_out):
    tm = x_ref.shape[1]
    parts = [pl.ds(i * (tm // IN_PARTS), tm // IN_PARTS) for i in range(IN_PARTS)]
    h = [_rms(x_ref[0, r, :], g_ref[...]).astype(BF16) for r in parts]

    def project(w_ref, out, epilogue=lambda y: y):
        for r, y in zip(parts, [_dot(v, w_ref[...]) for v in h]):
            out[0, r, :] = epilogue(y).astype(out.dtype)

    misc = [_dot(v, wmisc_ref[...]) for v in h]
    for r, m in zip(parts, misc):
        kpe_out[0, r, :] = _rope(m[:, :2 * MLA_ROPE], rope_ref[r, :]).astype(BF16)

    lat_cols = slice(2 * MLA_ROPE, 2 * MLA_ROPE + 2 * GLA_RANK)
    zs = [_dot(m[:, lat_cols].astype(BF16), wa_ref[...]) + ba_ref[...] for m in misc]
    qk = GLA_HEADS * GLA_DK
    for r, z in zip(parts, zs):
        la = (jnp.minimum(z, 0.0) - jnp.log(1.0 + jnp.exp(-jnp.abs(z)))) * (LOG2_E / GLA_TEMP)
        laf_out[0, r, :] = la[:, :qk]
        lab_out[0, r, :] = la[:, qk:]

    project(wv_ref, v_out)
    project(wg_ref, g_out)
    project(wga_ref, ga_out, jax.nn.sigmoid)
    project(wq_ref, q_out)
    project(wgb_ref, gb_out, jax.nn.sigmoid)
    rq = nq_ref.shape[1]
    for r, lat in zip(parts, [_dot(v, wlat_ref[...]) for v in h]):
        cq_out[0, r, :] = _rms(lat[:, :rq], nq_ref[...]).astype(BF16)
        ckv_out[0, r, :] = _rms(lat[:, rq:], nkv_ref[...]).astype(BF16)
    project(wk_ref, k_out)


def _in_proj(x, p):
    B, L, D = x.shape
    tm = min(IN_TOKENS, L)
    grid = (B, L // tm)
    tok = lambda n, dt=BF16: (pl.BlockSpec((1, tm, n), lambda b, j: (b, j, 0)),
                              jax.ShapeDtypeStruct((B, L, n), dt))
    outs = [tok(p['wq'].shape[1]), tok(p['wk'].shape[1]), tok(p['wv'].shape[1]),
            tok(p['wg'].shape[1]), tok(p['wq'].shape[1], F32), tok(p['wq'].shape[1], F32),
            tok(p['nq'].shape[1]), tok(p['nkv'].shape[1]), tok(MLA_ROPE), tok(D), tok(D)]
    weights = [p['wq'], p['wk'], p['wv'], p['wg'], p['wlat'], p['wmisc'], p['wga'], p['wgb'],
               p['wa'], p['ba']]
    return pl.pallas_call(
        _in_proj_kernel,
        grid=grid,
        in_specs=[pl.BlockSpec((1, tm, D), lambda b, j: (b, j, 0)),
                  _resident(p['g_pre'].shape),
                  pl.BlockSpec((tm, 2 * MLA_ROPE), lambda b, j: (j, 0)),
                  _resident(p['nq'].shape), _resident(p['nkv'].shape)]
                 + [_resident(w.shape) for w in weights],
        out_specs=[o[0] for o in outs],
        out_shape=[o[1] for o in outs],
        compiler_params=_params(),
        name="in_proj",
    )(x, p['g_pre'], p['rope'], p['nq'], p['nkv'], *weights)


def _gla_kernel(q_ref, k_ref, v_ref, g_ref, laf_ref, lab_ref, gn_ref, o_ref, *scratch):
    for hd in range(GLA_HEADS_PER_STEP):
        kc = pl.ds(hd * GLA_DK, GLA_DK)
        vc = pl.ds(hd * GLA_DV, GLA_DV)
        _gla_head(q_ref.at[:, :, kc], k_ref.at[:, :, kc], v_ref.at[:, :, vc], g_ref.at[:, :, vc],
                  laf_ref.at[:, :, kc], lab_ref.at[:, :, kc], gn_ref, o_ref.at[:, :, vc], *scratch)


def _gla_head(q_ref, k_ref, v_ref, g_ref, laf_ref, lab_ref,
              gn_ref, o_ref, mf_ref, mb_ref, sf_ref, sb_ref, bf_ref, bb_ref,
              qif_ref, qib_ref, uf_ref, ub_ref, dtf_ref, dtb_ref, *rest):
    L = q_ref.shape[1]
    C = GLA_CHUNK
    nc = L // C
    nr = min(NORM_ROWS, L)
    mid = C // 2 - 1
    scale = GLA_DK ** -0.5
    a_refs, oacc_ref = rest[:-1], rest[-1]
    group = len(a_refs)
    assert nc % group == 0

    row = lax.broadcasted_iota(jnp.int32, (C, C), 0)
    col = lax.broadcasted_iota(jnp.int32, (C, C), 1)
    mf_ref[...] = jnp.where(col <= row, 1.0, 0.0).astype(BF16)
    mb_ref[...] = jnp.where(col >= row, 1.0, 0.0).astype(BF16)

    def cumsum(c, carry):
        rows = pl.ds(pl.multiple_of(c * C, C), C)
        for m_ref, la_ref, b_ref in ((mf_ref, laf_ref, bf_ref), (mb_ref, lab_ref, bb_ref)):
            la = la_ref[0, rows, :]
            hi = la.astype(BF16)
            lo = (la - hi.astype(F32)).astype(BF16)
            cs = _dot(m_ref[...], jnp.concatenate([hi, lo], axis=1))
            b_ref[rows, :] = cs[:, :GLA_DK] + cs[:, GLA_DK:]
        return carry

    lax.fori_loop(0, nc, cumsum, 0, unroll=GLA_UNROLL)

    dirs = ((bf_ref, qif_ref, uf_ref, dtf_ref, C - 1, mid, True),
            (bb_ref, qib_ref, ub_ref, dtb_ref, 0, mid + 1, False))

    def chunk_rows(c):
        return pl.ds(c * C if isinstance(c, int) else pl.multiple_of(c * C, C), C)

    def intra_out(grp):
        for slot, a_ref in enumerate(a_refs):
            rows = chunk_rows(group * grp + slot)
            oacc_ref[rows, :] = _dot(a_ref[chunk_rows(grp), :], v_ref[0, rows, :])

    lagged = nc > group
    if lagged:
        for a_ref in a_refs:
            a_ref[:C, :] = jnp.zeros((C, C), BF16)

    def intra(grp, carry):
        if lagged:
            intra_out(jnp.maximum(grp - 1, 0))
        for slot, a_ref in enumerate(a_refs):
            rows = chunk_rows(group * grp + slot)
            q = q_ref[0, rows, :].astype(F32) * scale
            k = k_ref[0, rows, :].astype(F32)
            v = v_ref[0, rows, :]
            a_sum = None
            for b_ref, qi_ref, u_ref, dt_ref, tot_row, ref_row, fwd in dirs:
                b_incl = b_ref[rows, :]
                b_mid = b_incl[ref_row:ref_row + 1, :]
                b_tot = b_incl[tot_row:tot_row + 1, :]
                b_rel = b_incl - b_mid
                q_rel = q * jnp.exp2(b_rel)
                k_rel = k * jnp.exp2(-b_rel)
                qi_ref[rows, :] = (q_rel * jnp.exp2(b_mid)).astype(BF16)
                k_end = k_rel * jnp.exp2(b_tot - b_mid)
                mask = (row >= col) if fwd else (row <= col)
                a = jnp.where(mask, _dot_nt(q_rel.astype(BF16), k_rel.astype(BF16)), 0.0)
                a_sum = a if a_sum is None else a_sum + a
                u_ref[rows, :] = _dot(k_end.T.astype(BF16), v)
                dt_ref[rows, :] = jnp.broadcast_to(jnp.exp2(b_tot), (C, GLA_DK)).T
            a_ref[chunk_rows(grp), :] = a_sum.astype(BF16)
        return carry

    if lagged:
        lax.fori_loop(0, nc // group, intra, 0)
    else:
        intra(0, 0)
    intra_out(nc // group - 1)

    sf_ref[...] = jnp.zeros_like(sf_ref)
    sb_ref[...] = jnp.zeros_like(sb_ref)

    def scan(i, carry):
        for s_ref, qi_ref, u_ref, dt_ref, c in ((sf_ref, qif_ref, uf_ref, dtf_ref, i),
                                                (sb_ref, qib_ref, ub_ref, dtb_ref, nc - 1 - i)):
            rows = pl.ds(pl.multiple_of(c * C, C), C)
            s = s_ref[...]
            oacc_ref[rows, :] += _dot(qi_ref[rows, :], s.astype(BF16))
            dt = dt_ref[rows, :]
            s_ref[...] = s * jnp.concatenate([dt] * (GLA_DV // C), axis=1) + u_ref[rows, :]
        return carry

    lax.fori_loop(0, nc, scan, 0, unroll=GLA_UNROLL)

    for r in range(L // nr):
        rows = pl.ds(r * nr, nr)
        y = _rms(oacc_ref[rows, :], gn_ref[...])
        g = g_ref[0, rows, :].astype(F32)
        o_ref[0, rows, :] = (y * _silu(g)).astype(BF16)


def _gla(q, k, v, g, la_f, la_b, p):
    B, L, _ = q.shape
    C = GLA_CHUNK
    group = min(INTRA_GROUP, L // C)
    hps = GLA_HEADS_PER_STEP
    head = lambda n: pl.BlockSpec((1, L, hps * n), lambda b, h: (b, 0, h))
    return pl.pallas_call(
        _gla_kernel,
        grid=(B, GLA_HEADS // hps),
        in_specs=[head(GLA_DK), head(GLA_DK), head(GLA_DV), head(GLA_DV),
                  head(GLA_DK), head(GLA_DK),
                  pl.BlockSpec((1, GLA_DV), lambda b, h: (0, 0))],
        out_specs=head(GLA_DV),
        out_shape=jax.ShapeDtypeStruct((B, L, GLA_HEADS * GLA_DV), BF16),
        scratch_shapes=[pltpu.VMEM((C, C), BF16), pltpu.VMEM((C, C), BF16),
                        pltpu.VMEM((GLA_DK, GLA_DV), F32), pltpu.VMEM((GLA_DK, GLA_DV), F32),
                        pltpu.VMEM((L, GLA_DK), F32), pltpu.VMEM((L, GLA_DK), F32),
                        pltpu.VMEM((L, GLA_DK), BF16), pltpu.VMEM((L, GLA_DK), BF16),
                        pltpu.VMEM((L, GLA_DV), F32), pltpu.VMEM((L, GLA_DV), F32),
                        pltpu.VMEM((L, GLA_DK), F32), pltpu.VMEM((L, GLA_DK), F32),
                        *[pltpu.VMEM((L // group, C), BF16) for _ in range(group)],
                        pltpu.VMEM((L, GLA_DV), F32)],
        compiler_params=_params(),
        name="gla",
    )(q, k, v, g, la_f, la_b, p['gla_norm'])


def _mla_kernel(cq_ref, ckv_ref, kpe_ref, rope_ref, wq_ref, wkv_ref, o_ref, *scratch):
    L = cq_ref.shape[1]
    scale = (MLA_NOPE + MLA_ROPE) ** -0.5 * LOG2_E
    tq = min(Q_TOKENS, L)
    nq = L // tq

    heads = [scratch[3 * hd:3 * hd + 3] for hd in range(MLA_HEADS_PER_STEP)]
    for hd, (qs_ref, ks_ref, vs_ref) in enumerate(heads):
        qall = _dot(cq_ref[0], wq_ref[hd])
        qs_ref[:, :MLA_NOPE] = (qall[:, :MLA_NOPE] * scale).astype(BF16)
        qs_ref[:, MLA_NOPE:] = (_rope(qall[:, MLA_NOPE:], rope_ref[...]) * scale).astype(BF16)
        kv = _dot(ckv_ref[0], wkv_ref[hd])
        ks_ref[:, :MLA_NOPE] = kv[:, :MLA_NOPE].astype(BF16)
        ks_ref[:, MLA_NOPE:] = kpe_ref[0]
        vs_ref[:, :MLA_V] = kv[:, MLA_NOPE:].astype(BF16)
        vs_ref[:, MLA_V:] = jnp.ones((L, MXU_WIDTH_V7X - MLA_V), BF16)

    items = [(hd, i) for hd in range(MLA_HEADS_PER_STEP) for i in range(nq)]

    def scores(item):
        hd, i = item
        qs_ref, ks_ref, _ = heads[hd]
        return _dot_nt(qs_ref[i * tq:(i + 1) * tq, :], ks_ref[...])

    pending = [scores(it) for it in items[:Q_AHEAD]]
    for n, (hd, i) in enumerate(items):
        if n + Q_AHEAD < len(items):
            pending.append(scores(items[n + Q_AHEAD]))
        s = pending.pop(0)
        e = jnp.exp2(s - jnp.max(s, axis=-1, keepdims=True)).astype(BF16)
        o_ext = _dot(e, heads[hd][2][...])
        o = o_ext[:, :MLA_V] * (1.0 / o_ext[:, MLA_V:MLA_V + 1])
        o_ref[0, i * tq:(i + 1) * tq, hd * MLA_V:(hd + 1) * MLA_V] = o.astype(BF16)


def _mla(cq, ckv, kpe, p):
    B, L, R = cq.shape
    dqk = MLA_NOPE + MLA_ROPE
    hps = MLA_HEADS_PER_STEP
    seq = lambda n: pl.BlockSpec((1, L, n), lambda b, h: (b, 0, 0))
    per_step = lambda a: pl.BlockSpec((hps,) + a.shape[1:], lambda b, h: (h, 0, 0))
    return pl.pallas_call(
        _mla_kernel,
        grid=(B, MLA_HEADS // hps),
        in_specs=[seq(R), seq(ckv.shape[2]), seq(MLA_ROPE), _resident(p['rope'].shape),
                  per_step(p['wq_mla']), per_step(p['wkv_mla'])],
        out_specs=pl.BlockSpec((1, L, hps * MLA_V), lambda b, h: (b, 0, h)),
        out_shape=jax.ShapeDtypeStruct((B, L, MLA_HEADS * MLA_V), BF16),
        scratch_shapes=[pltpu.VMEM((L, dqk), BF16), pltpu.VMEM((L, dqk), BF16),
                        pltpu.VMEM((L, MXU_WIDTH_V7X), BF16)] * hps,
        compiler_params=_params(),
        name="mla",
    )(cq, ckv, kpe, p['rope'], p['wq_mla'], p['wkv_mla'])


def _out_kernel(x_ref, oa_ref, ob_ref, ga_ref, gb_ref, woa_ref, wob_ref, wout_ref,
                wgate_ref, wup_ref, wdown_ref, npost_ref, nfpre_ref, nfpost_ref, y_ref):
    tm = x_ref.shape[1]
    parts = [pl.ds(i * (tm // OUT_PARTS), tm // OUT_PARTS) for i in range(OUT_PARTS)]
    ya = [_dot(oa_ref[0, r, :], woa_ref[...]) for r in parts]
    yb = [_dot(ob_ref[0, r, :], wob_ref[...]) for r in parts]
    merged = [(ga_ref[0, r, :].astype(F32) * a + gb_ref[0, r, :].astype(F32) * b).astype(BF16)
              for r, a, b in zip(parts, ya, yb)]
    mo = [_dot(m, wout_ref[...]) for m in merged]
    x1 = [x_ref[0, r, :] + _rms(o, npost_ref[...]) for r, o in zip(parts, mo)]
    h = [_rms(x, nfpre_ref[...]).astype(BF16) for x in x1]
    gate = [_dot(v, wgate_ref[...]) for v in h]
    up = [_dot(v, wup_ref[...]) for v in h]
    act = [(_silu(g) * u).astype(BF16) for g, u in zip(gate, up)]
    down = [_dot(a, wdown_ref[...]) for a in act]
    for r, x, d in zip(parts, x1, down):
        y_ref[0, r, :] = x + _rms(d, nfpost_ref[...])


def _out(x, oa, ob, ga, gb, p):
    B, L, D = x.shape
    tm = min(OUT_TOKENS, L)
    tok = pl.BlockSpec((1, tm, D), lambda b, j: (b, j, 0))
    consts = [p['w_o_gla'], p['w_o_mla'], p['w_out'], p['w_gate'], p['w_up'], p['w_down'],
              p['g_post'], p['g_ffn_pre'], p['g_ffn_post']]
    return pl.pallas_call(
        _out_kernel,
        grid=(B, L // tm),
        in_specs=[tok] * 5 + [_resident(c.shape) for c in consts],
        out_specs=tok,
        out_shape=jax.ShapeDtypeStruct((B, L, D), F32),
        compiler_params=_params(),
        name="merge_ffn",
    )(x, oa, ob, ga, gb, *consts)


def _prepare(L, norm_mix_pre, w_in, gla_wa_fwd, gla_ba_fwd, gla_wa_bwd, gla_ba_bwd, gla_norm,
             w_o_gla, mla_norm_q, w_uq, mla_norm_kv, w_uk, w_uv, w_o_mla, w_out,
             norm_mix_post, norm_ffn_pre, w_gate, w_up, w_down, norm_ffn_post):
    D = w_in.shape[0]
    qk = GLA_HEADS * GLA_DK
    vw = GLA_HEADS * GLA_DV
    rq, rkv = mla_norm_q.shape[0], mla_norm_kv.shape[0]
    widths = (qk, qk, vw, vw, GLA_RANK, GLA_RANK, rq, rkv, MLA_ROPE, D, D)
    offs = np.concatenate([[0], np.cumsum(widths)]).tolist()
    assert w_in.shape[1] == offs[-1], "combined input projection does not match the layer layout"
    assert gla_wa_fwd.shape == gla_wa_bwd.shape == (GLA_RANK, qk) and gla_norm.shape == (GLA_DV,)
    assert w_uq.shape == (rq, MLA_HEADS * (MLA_NOPE + MLA_ROPE))
    assert w_uk.shape == (rkv, MLA_HEADS * MLA_NOPE) and w_uv.shape == (rkv, MLA_HEADS * MLA_V)
    assert w_o_gla.shape == (vw, D) and w_o_mla.shape == (MLA_HEADS * MLA_V, D)
    assert all(L % min(t, L) == 0 for t in (IN_TOKENS, OUT_TOKENS, Q_TOKENS, GLA_CHUNK))
    piece = lambda i, j=None: w_in[:, offs[i]:offs[(i if j is None else j) + 1]]
    swap = np.concatenate([np.arange(MLA_ROPE // 2, MLA_ROPE), np.arange(MLA_ROPE // 2)])
    w_kr = piece(8)
    row = lambda a: a.reshape(1, -1).astype(F32)

    inv = ROPE_BASE ** (-jnp.arange(0, MLA_ROPE, 2, dtype=F32) / MLA_ROPE)
    ang = jnp.arange(L, dtype=F32)[:, None] * inv[None, :]
    cos, sin = jnp.cos(ang), jnp.sin(ang)

    uq = w_uq.reshape(rq, MLA_HEADS, MLA_NOPE + MLA_ROPE)
    wq_pe = uq[:, :, MLA_NOPE:]
    wq_mla = jnp.concatenate([uq[:, :, :MLA_NOPE], wq_pe, wq_pe[:, :, swap]], axis=2)
    wkv_mla = jnp.concatenate([w_uk.reshape(rkv, MLA_HEADS, MLA_NOPE),
                               w_uv.reshape(rkv, MLA_HEADS, MLA_V)], axis=2)
    zeros = jnp.zeros((GLA_RANK, qk), F32)
    return dict(
        g_pre=row(norm_mix_pre),
        wq=piece(0).astype(BF16), wk=piece(1).astype(BF16), wv=piece(2).astype(BF16),
        wg=piece(3).astype(BF16), wlat=piece(6, 7).astype(BF16),
        wmisc=jnp.concatenate([w_kr, w_kr[:, swap], piece(4, 5)], axis=1).astype(BF16),
        wga=piece(9).astype(BF16), wgb=piece(10).astype(BF16),
        nq=row(mla_norm_q), nkv=row(mla_norm_kv),
        rope=jnp.concatenate([cos, cos, -sin, sin], axis=1),
        wa=jnp.concatenate([jnp.concatenate([gla_wa_fwd, zeros], axis=1),
                            jnp.concatenate([zeros, gla_wa_bwd], axis=1)], axis=0).astype(BF16),
        ba=jnp.concatenate([row(gla_ba_fwd), row(gla_ba_bwd)], axis=1), gla_norm=row(gla_norm),
        wq_mla=jnp.transpose(wq_mla, (1, 0, 2)).astype(BF16),
        wkv_mla=jnp.transpose(wkv_mla, (1, 0, 2)).astype(BF16),
        w_o_gla=w_o_gla.astype(BF16), w_o_mla=w_o_mla.astype(BF16), w_out=w_out.astype(BF16),
        w_gate=w_gate.astype(BF16), w_up=w_up.astype(BF16), w_down=w_down.astype(BF16),
        g_post=row(norm_mix_post), g_ffn_pre=row(norm_ffn_pre), g_ffn_post=row(norm_ffn_post),
    )


def _layer(x, p):
    q, k, v, g, la_f, la_b, cq, ckv, kpe, ga, gb = _in_proj(x, p)
    oa = _gla(q, k, v, g, la_f, la_b, p)
    ob = _mla(cq, ckv, kpe, p)
    return _out(x, oa, ob, ga, gb, p)


def kernel(x_prompt, x_sample, norm_mix_pre, w_in, gla_wa_fwd, gla_ba_fwd, gla_wa_bwd, gla_ba_bwd, gla_norm, w_o_gla, mla_norm_q, w_uq, mla_norm_kv, w_uk, w_uv, w_o_mla, w_out, norm_mix_post, norm_ffn_pre, w_gate, w_up, w_down, norm_ffn_post):
    assert x_prompt.shape[1] == x_sample.shape[1]
    stacked = (norm_mix_pre, w_in, gla_wa_fwd, gla_ba_fwd, gla_wa_bwd, gla_ba_bwd, gla_norm,
               w_o_gla, mla_norm_q, w_uq, mla_norm_kv, w_uk, w_uv, w_o_mla, w_out,
               norm_mix_post, norm_ffn_pre, w_gate, w_up, w_down, norm_ffn_post)
    y_prompt, y_sample = x_prompt, x_sample
    for layer in range(norm_mix_pre.shape[0]):
        p = _prepare(x_prompt.shape[1], *[a[layer] for a in stacked])
        y_prompt, y_sample = _layer(y_prompt, p), _layer(y_sample, p)
    return (y_prompt, y_sample)
```

```python
import jax
import jax.numpy as jnp
import numpy as np
from jax import lax
from jax.experimental import pallas as pl
from jax.experimental.pallas import tpu as pltpu

F32 = jnp.float32
BF16 = jnp.bfloat16

EPS = 1e-6
GLA_HEADS = 4
GLA_DK = 128
GLA_DV = 256
GLA_RANK = 16
GLA_TEMP = 16.0
GLA_CHUNK = 128
MLA_HEADS = 8
MLA_NOPE = 128
MLA_ROPE = 64
MLA_V = 128
ROPE_BASE = 10000.0

IN_TOKENS = 512
IN_PARTS = 2
OUT_TOKENS = 512
OUT_PARTS = 2
Q_TOKENS = 256
Q_AHEAD = 1
MLA_HEADS_PER_STEP = 2
MXU_WIDTH_V7X = 256
LOG2_E = 1.4426950408889634
GLA_UNROLL = 16
INTRA_GROUP = 16
GLA_HEADS_PER_STEP = 2
NORM_ROWS = 256
FF_CHUNK = 1536
VMEM_LIMIT_BYTES = 56 * 1024 * 1024


def _rms(xf, g):
    ms = jnp.mean(xf * xf, axis=-1, keepdims=True)
    return xf * lax.rsqrt(ms + EPS) * g


def _silu(x):
    h = 0.5 * x
    return h + h * jnp.tanh(h)


def _dot(a, b):
    return jnp.dot(a, b, preferred_element_type=F32)


def _dot_nt(a, b):
    return lax.dot_general(a, b, (((1,), (1,)), ((), ())), preferred_element_type=F32)


def _rope(x2, table):
    y = x2 * table
    return (y + pltpu.roll(y, MLA_ROPE, axis=1))[:, :MLA_ROPE]


def _params():
    return pltpu.CompilerParams(
        dimension_semantics=("arbitrary", "arbitrary"),
        vmem_limit_bytes=VMEM_LIMIT_BYTES,
    )


def _resident(shape):
    nd = len(shape)
    return pl.BlockSpec(shape, lambda *_: (0,) * nd, pipeline_mode=pl.Buffered(1))


def _in_proj_kernel(x_ref, g_ref, rope_ref, nq_ref, nkv_ref,
                    wq_ref, wk_ref, wv_ref, wg_ref, wlat_ref, wmisc_ref, wga_ref, wgb_ref,
                    wa_ref, ba_ref,
                    q_out, k_out, v_out, g_out, laf_out, lab_out, cq_out, ckv_out, kpe_out,
                    ga_out, gb_out):
    tm = x_ref.shape[1]
    parts = [pl.ds(i * (tm // IN_PARTS), tm // IN_PARTS) for i in range(IN_PARTS)]
    h = [_rms(x_ref[0, r, :], g_ref[...]).astype(BF16) for r in parts]

    def project(w_ref, out, epilogue=lambda y: y):
        for r, y in zip(parts, [_dot(v, w_ref[...]) for v in h]):
            out[0, r, :] = epilogue(y).astype(out.dtype)

    misc = [_dot(v, wmisc_ref[...]) for v in h]
    for r, m in zip(parts, misc):
        kpe_out[0, r, :] = _rope(m[:, :2 * MLA_ROPE], rope_ref[r, :]).astype(BF16)

    lat_cols = slice(2 * MLA_ROPE, 2 * MLA_ROPE + 2 * GLA_RANK)
    zs = [_dot(m[:, lat_cols].astype(BF16), wa_ref[...]) + ba_ref[...] for m in misc]
    qk = GLA_HEADS * GLA_DK
    for r, z in zip(parts, zs):
        la = (jnp.minimum(z, 0.0) - jnp.log(1.0 + jnp.exp(-jnp.abs(z)))) * (LOG2_E / GLA_TEMP)
        laf_out[0, r, :] = la[:, :qk]
        lab_out[0, r, :] = la[:, qk:]

    project(wv_ref, v_out)
    project(wg_ref, g_out)
    project(wga_ref, ga_out, jax.nn.sigmoid)
    project(wq_ref, q_out)
    project(wgb_ref, gb_out, jax.nn.sigmoid)
    rq = nq_ref.shape[1]
    for r, lat in zip(parts, [_dot(v, wlat_ref[...]) for v in h]):
        cq_out[0, r, :] = _rms(lat[:, :rq], nq_ref[...]).astype(BF16)
        ckv_out[0, r, :] = _rms(lat[:, rq:], nkv_ref[...]).astype(BF16)
    project(wk_ref, k_out)


def _in_proj(x, p):
    B, L, D = x.shape
    tm = min(IN_TOKENS, L)
    grid = (B, L // tm)
    tok = lambda n, dt=BF16: (pl.BlockSpec((1, tm, n), lambda b, j: (b, j, 0)),
                              jax.ShapeDtypeStruct((B, L, n), dt))
    outs = [tok(p['wq'].shape[1]), tok(p['wk'].shape[1]), tok(p['wv'].shape[1]),
            tok(p['wg'].shape[1]), tok(p['wq'].shape[1], F32), tok(p['wq'].shape[1], F32),
            tok(p['nq'].shape[1]), tok(p['nkv'].shape[1]), tok(MLA_ROPE), tok(D), tok(D)]
    weights = [p['wq'], p['wk'], p['wv'], p['wg'], p['wlat'], p['wmisc'], p['wga'], p['wgb'],
               p['wa'], p['ba']]
    return pl.pallas_call(
        _in_proj_kernel,
        grid=grid,
        in_specs=[pl.BlockSpec((1, tm, D), lambda b, j: (b, j, 0)),
                  _resident(p['g_pre'].shape),
                  pl.BlockSpec((tm, 2 * MLA_ROPE), lambda b, j: (j, 0)),
                  _resident(p['nq'].shape), _resident(p['nkv'].shape)]
                 + [_resident(w.shape) for w in weights],
        out_specs=[o[0] for o in outs],
        out_shape=[o[1] for o in outs],
        compiler_params=_params(),
        name="in_proj",
    )(x, p['g_pre'], p['rope'], p['nq'], p['nkv'], *weights)


def _gla_kernel(q_ref, k_ref, v_ref, g_ref, laf_ref, lab_ref, gn_ref, o_ref, *scratch):
    for hd in range(GLA_HEADS_PER_STEP):
        kc = pl.ds(hd * GLA_DK, GLA_DK)
        vc = pl.ds(hd * GLA_DV, GLA_DV)
        _gla_head(q_ref.at[:, :, kc], k_ref.at[:, :, kc], v_ref.at[:, :, vc], g_ref.at[:, :, vc],
                  laf_ref.at[:, :, kc], lab_ref.at[:, :, kc], gn_ref, o_ref.at[:, :, vc], *scratch)


def _gla_head(q_ref, k_ref, v_ref, g_ref, laf_ref, lab_ref,
              gn_ref, o_ref, mf_ref, mb_ref, sf_ref, sb_ref, bf_ref, bb_ref,
              qif_ref, qib_ref, uf_ref, ub_ref, dtf_ref, dtb_ref, *rest):
    L = q_ref.shape[1]
    C = GLA_CHUNK
    nc = L // C
    nr = min(NORM_ROWS, L)
    mid = C // 2 - 1
    scale = GLA_DK ** -0.5
    a_refs, oacc_ref = rest[:-1], rest[-1]
    group = len(a_refs)
    assert nc % group == 0

    row = lax.broadcasted_iota(jnp.int32, (C, C), 0)
    col = lax.broadcasted_iota(jnp.int32, (C, C), 1)
    mf_ref[...] = jnp.where(col <= row, 1.0, 0.0).astype(BF16)
    mb_ref[...] = jnp.where(col >= row, 1.0, 0.0).astype(BF16)

    def cumsum(c, carry):
        rows = pl.ds(pl.multiple_of(c * C, C), C)
        for m_ref, la_ref, b_ref in ((mf_ref, laf_ref, bf_ref), (mb_ref, lab_ref, bb_ref)):
            la = la_ref[0, rows, :]
            hi = la.astype(BF16)
            lo = (la - hi.astype(F32)).astype(BF16)
            cs = _dot(m_ref[...], jnp.concatenate([hi, lo], axis=1))
            b_ref[rows, :] = cs[:, :GLA_DK] + cs[:, GLA_DK:]
        return carry

    lax.fori_loop(0, nc, cumsum, 0, unroll=GLA_UNROLL)

    dirs = ((bf_ref, qif_ref, uf_ref, dtf_ref, C - 1, mid, True),
            (bb_ref, qib_ref, ub_ref, dtb_ref, 0, mid + 1, False))

    def chunk_rows(c):
        return pl.ds(c * C if isinstance(c, int) else pl.multiple_of(c * C, C), C)

    def intra_out(grp):
        for slot, a_ref in enumerate(a_refs):
            rows = chunk_rows(group * grp + slot)
            oacc_ref[rows, :] = _dot(a_ref[chunk_rows(grp), :], v_ref[0, rows, :])

    lagged = nc > group
    if lagged:
        for a_ref in a_refs:
            a_ref[:C, :] = jnp.zeros((C, C), BF16)

    def intra(grp, carry):
        if lagged:
            intra_out(jnp.maximum(grp - 1, 0))
        for slot, a_ref in enumerate(a_refs):
            rows = chunk_rows(group * grp + slot)
            q = q_ref[0, rows, :].astype(F32) * scale
            k = k_ref[0, rows, :].astype(F32)
            v = v_ref[0, rows, :]
            a_sum = None
            for b_ref, qi_ref, u_ref, dt_ref, tot_row, ref_row, fwd in dirs:
                b_incl = b_ref[rows, :]
                b_mid = b_incl[ref_row:ref_row + 1, :]
                b_tot = b_incl[tot_row:tot_row + 1, :]
                b_rel = b_incl - b_mid
                q_rel = q * jnp.exp2(b_rel)
                k_rel = k * jnp.exp2(-b_rel)
                qi_ref[rows, :] = (q_rel * jnp.exp2(b_mid)).astype(BF16)
                k_end = k_rel * jnp.exp2(b_tot - b_mid)
                mask = (row >= col) if fwd else (row <= col)
                a = jnp.where(mask, _dot_nt(q_rel.astype(BF16), k_rel.astype(BF16)), 0.0)
                a_sum = a if a_sum is None else a_sum + a
                u_ref[rows, :] = _dot(k_end.T.astype(BF16), v)
                dt_ref[rows, :] = jnp.broadcast_to(jnp.exp2(b_tot), (C, GLA_DK)).T
            a_ref[chunk_rows(grp), :] = a_sum.astype(BF16)
        return carry

    if lagged:
        lax.fori_loop(0, nc // group, intra, 0)
    else:
        intra(0, 0)
    intra_out(nc // group - 1)

    sf_ref[...] = jnp.zeros_like(sf_ref)
    sb_ref[...] = jnp.zeros_like(sb_ref)

    def scan(i, carry):
        for s_ref, qi_ref, u_ref, dt_ref, c in ((sf_ref, qif_ref, uf_ref, dtf_ref, i),
                                                (sb_ref, qib_ref, ub_ref, dtb_ref, nc - 1 - i)):
            rows = pl.ds(pl.multiple_of(c * C, C), C)
            s = s_ref[...]
            oacc_ref[rows, :] += _dot(qi_ref[rows, :], s.astype(BF16))
            dt = dt_ref[rows, :]
            s_ref[...] = s * jnp.concatenate([dt] * (GLA_DV // C), axis=1) + u_ref[rows, :]
        return carry

    lax.fori_loop(0, nc, scan, 0, unroll=GLA_UNROLL)

    for r in range(L // nr):
        rows = pl.ds(r * nr, nr)
        y = _rms(oacc_ref[rows, :], gn_ref[...])
        g = g_ref[0, rows, :].astype(F32)
        o_ref[0, rows, :] = (y * _silu(g)).astype(BF16)


def _gla(q, k, v, g, la_f, la_b, p):
    B, L, _ = q.shape
    C = GLA_CHUNK
    group = min(INTRA_GROUP, L // C)
    hps = GLA_HEADS_PER_STEP
    head = lambda n: pl.BlockSpec((1, L, hps * n), lambda b, h: (b, 0, h))
    return pl.pallas_call(
        _gla_kernel,
        grid=(B, GLA_HEADS // hps),
        in_specs=[head(GLA_DK), head(GLA_DK), head(GLA_DV), head(GLA_DV),
                  head(GLA_DK), head(GLA_DK),
                  pl.BlockSpec((1, GLA_DV), lambda b, h: (0, 0))],
        out_specs=head(GLA_DV),
        out_shape=jax.ShapeDtypeStruct((B, L, GLA_HEADS * GLA_DV), BF16),
        scratch_shapes=[pltpu.VMEM((C, C), BF16), pltpu.VMEM((C, C), BF16),
                        pltpu.VMEM((GLA_DK, GLA_DV), F32), pltpu.VMEM((GLA_DK, GLA_DV), F32),
                        pltpu.VMEM((L, GLA_DK), F32), pltpu.VMEM((L, GLA_DK), F32),
                        pltpu.VMEM((L, GLA_DK), BF16), pltpu.VMEM((L, GLA_DK), BF16),
                        pltpu.VMEM((L, GLA_DV), F32), pltpu.VMEM((L, GLA_DV), F32),
                        pltpu.VMEM((L, GLA_DK), F32), pltpu.VMEM((L, GLA_DK), F32),
                        *[pltpu.VMEM((L // group, C), BF16) for _ in range(group)],
                        pltpu.VMEM((L, GLA_DV), F32)],
        compiler_params=_params(),
        name="gla",
    )(q, k, v, g, la_f, la_b, p['gla_norm'])


def _mla_kernel(cq_ref, ckv_ref, kpe_ref, rope_ref, wq_ref, wkv_ref, o_ref, *scratch):
    L = cq_ref.shape[1]
    scale = (MLA_NOPE + MLA_ROPE) ** -0.5 * LOG2_E
    tq = min(Q_TOKENS, L)
    nq = L // tq

    heads = [scratch[3 * hd:3 * hd + 3] for hd in range(MLA_HEADS_PER_STEP)]
    for hd, (qs_ref, ks_ref, vs_ref) in enumerate(heads):
        qall = _dot(cq_ref[0], wq_ref[hd])
        qs_ref[:, :MLA_NOPE] = (qall[:, :MLA_NOPE] * scale).astype(BF16)
        qs_ref[:, MLA_NOPE:] = (_rope(qall[:, MLA_NOPE:], rope_ref[...]) * scale).astype(BF16)
        kv = _dot(ckv_ref[0], wkv_ref[hd])
        ks_ref[:, :MLA_NOPE] = kv[:, :MLA_NOPE].astype(BF16)
        ks_ref[:, MLA_NOPE:] = kpe_ref[0]
        vs_ref[:, :MLA_V] = kv[:, MLA_NOPE:].astype(BF16)
        vs_ref[:, MLA_V:] = jnp.ones((L, MXU_WIDTH_V7X - MLA_V), BF16)

    items = [(hd, i) for hd in range(MLA_HEADS_PER_STEP) for i in range(nq)]

    def scores(item):
        hd, i = item
        qs_ref, ks_ref, _ = heads[hd]
        return _dot_nt(qs_ref[i * tq:(i + 1) * tq, :], ks_ref[...])

    pending = [scores(it) for it in items[:Q_AHEAD]]
    for n, (hd, i) in enumerate(items):
        if n + Q_AHEAD < len(items):
            pending.append(scores(items[n + Q_AHEAD]))
        s = pending.pop(0)
        e = jnp.exp2(s - jnp.max(s, axis=-1, keepdims=True)).astype(BF16)
        o_ext = _dot(e, heads[hd][2][...])
        o = o_ext[:, :MLA_V] * (1.0 / o_ext[:, MLA_V:MLA_V + 1])
        o_ref[0, i * tq:(i + 1) * tq, hd * MLA_V:(hd + 1) * MLA_V] = o.astype(BF16)


def _mla(cq, ckv, kpe, p):
    B, L, R = cq.shape
    dqk = MLA_NOPE + MLA_ROPE
    hps = MLA_HEADS_PER_STEP
    seq = lambda n: pl.BlockSpec((1, L, n), lambda b, h: (b, 0, 0))
    per_step = lambda a: pl.BlockSpec((hps,) + a.shape[1:], lambda b, h: (h, 0, 0))
    return pl.pallas_call(
        _mla_kernel,
        grid=(B, MLA_HEADS // hps),
        in_specs=[seq(R), seq(ckv.shape[2]), seq(MLA_ROPE), _resident(p['rope'].shape),
                  per_step(p['wq_mla']), per_step(p['wkv_mla'])],
        out_specs=pl.BlockSpec((1, L, hps * MLA_V), lambda b, h: (b, 0, h)),
        out_shape=jax.ShapeDtypeStruct((B, L, MLA_HEADS * MLA_V), BF16),
        scratch_shapes=[pltpu.VMEM((L, dqk), BF16), pltpu.VMEM((L, dqk), BF16),
                        pltpu.VMEM((L, MXU_WIDTH_V7X), BF16)] * hps,
        compiler_params=_params(),
        name="mla",
    )(cq, ckv, kpe, p['rope'], p['wq_mla'], p['wkv_mla'])


def _out_kernel(x_ref, oa_ref, ob_ref, ga_ref, gb_ref, woa_ref, wob_ref, wout_ref,
                wgate_ref, wup_ref, wdown_ref, npost_ref, nfpre_ref, nfpost_ref, y_ref):
    tm = x_ref.shape[1]
    parts = [pl.ds(i * (tm // OUT_PARTS), tm // OUT_PARTS) for i in range(OUT_PARTS)]
    ya = [_dot(oa_ref[0, r, :], woa_ref[...]) for r in parts]
    yb = [_dot(ob_ref[0, r, :], wob_ref[...]) for r in parts]
    merged = [(ga_ref[0, r, :].astype(F32) * a + gb_ref[0, r, :].astype(F32) * b).astype(BF16)
              for r, a, b in zip(parts, ya, yb)]
    mo = [_dot(m, wout_ref[...]) for m in merged]
    x1 = [x_ref[0, r, :] + _rms(o, npost_ref[...]) for r, o in zip(parts, mo)]
    h = [_rms(x, nfpre_ref[...]).astype(BF16) for x in x1]
    d_ff = wgate_ref.shape[1]
    bounds = [0] + [min((c + 1) * FF_CHUNK, d_ff) for c in range(pl.cdiv(d_ff, FF_CHUNK))]
    down = [None] * len(parts)
    for lo, hi in zip(bounds[:-1], bounds[1:]):
        gate = [_dot(v, wgate_ref[:, lo:hi]) for v in h]
        up = [_dot(v, wup_ref[:, lo:hi]) for v in h]
        act = [(_silu(g) * u).astype(BF16) for g, u in zip(gate, up)]
        part = [_dot(a, wdown_ref[lo:hi, :]) for a in act]
        down = [d if acc is None else acc + d for acc, d in zip(down, part)]
    for r, x, d in zip(parts, x1, down):
        y_ref[0, r, :] = x + _rms(d, nfpost_ref[...])


def _out(x, oa, ob, ga, gb, p):
    B, L, D = x.shape
    tm = min(OUT_TOKENS, L)
    tok = pl.BlockSpec((1, tm, D), lambda b, j: (b, j, 0))
    consts = [p['w_o_gla'], p['w_o_mla'], p['w_out'], p['w_gate'], p['w_up'], p['w_down'],
              p['g_post'], p['g_ffn_pre'], p['g_ffn_post']]
    return pl.pallas_call(
        _out_kernel,
        grid=(B, L // tm),
        in_specs=[tok] * 5 + [_resident(c.shape) for c in consts],
        out_specs=tok,
        out_shape=jax.ShapeDtypeStruct((B, L, D), F32),
        compiler_params=_params(),
        name="merge_ffn",
    )(x, oa, ob, ga, gb, *consts)


def _prepare(L, norm_mix_pre, w_in, gla_wa_fwd, gla_ba_fwd, gla_wa_bwd, gla_ba_bwd, gla_norm,
             w_o_gla, mla_norm_q, w_uq, mla_norm_kv, w_uk, w_uv, w_o_mla, w_out,
             norm_mix_post, norm_ffn_pre, w_gate, w_up, w_down, norm_ffn_post):
    D = w_in.shape[0]
    qk = GLA_HEADS * GLA_DK
    vw = GLA_HEADS * GLA_DV
    rq, rkv = mla_norm_q.shape[0], mla_norm_kv.shape[0]
    widths = (qk, qk, vw, vw, GLA_RANK, GLA_RANK, rq, rkv, MLA_ROPE, D, D)
    offs = np.concatenate([[0], np.cumsum(widths)]).tolist()
    assert w_in.shape[1] == offs[-1], "combined input projection does not match the layer layout"
    assert gla_wa_fwd.shape == gla_wa_bwd.shape == (GLA_RANK, qk) and gla_norm.shape == (GLA_DV,)
    assert w_uq.shape == (rq, MLA_HEADS * (MLA_NOPE + MLA_ROPE))
    assert w_uk.shape == (rkv, MLA_HEADS * MLA_NOPE) and w_uv.shape == (rkv, MLA_HEADS * MLA_V)
    assert w_o_gla.shape == (vw, D) and w_o_mla.shape == (MLA_HEADS * MLA_V, D)
    assert all(L % min(t, L) == 0 for t in (IN_TOKENS, OUT_TOKENS, Q_TOKENS, GLA_CHUNK))
    piece = lambda i, j=None: w_in[:, offs[i]:offs[(i if j is None else j) + 1]]
    swap = np.concatenate([np.arange(MLA_ROPE // 2, MLA_ROPE), np.arange(MLA_ROPE // 2)])
    w_kr = piece(8)
    row = lambda a: a.reshape(1, -1).astype(F32)

    inv = ROPE_BASE ** (-jnp.arange(0, MLA_ROPE, 2, dtype=F32) / MLA_ROPE)
    ang = jnp.arange(L, dtype=F32)[:, None] * inv[None, :]
    cos, sin = jnp.cos(ang), jnp.sin(ang)

    uq = w_uq.reshape(rq, MLA_HEADS, MLA_NOPE + MLA_ROPE)
    wq_pe = uq[:, :, MLA_NOPE:]
    wq_mla = jnp.concatenate([uq[:, :, :MLA_NOPE], wq_pe, wq_pe[:, :, swap]], axis=2)
    wkv_mla = jnp.concatenate([w_uk.reshape(rkv, MLA_HEADS, MLA_NOPE),
                               w_uv.reshape(rkv, MLA_HEADS, MLA_V)], axis=2)
    zeros = jnp.zeros((GLA_RANK, qk), F32)
    return dict(
        g_pre=row(norm_mix_pre),
        wq=piece(0).astype(BF16), wk=piece(1).astype(BF16), wv=piece(2).astype(BF16),
        wg=piece(3).astype(BF16), wlat=piece(6, 7).astype(BF16),
        wmisc=jnp.concatenate([w_kr, w_kr[:, swap], piece(4, 5)], axis=1).astype(BF16),
        wga=piece(9).astype(BF16), wgb=piece(10).astype(BF16),
        nq=row(mla_norm_q), nkv=row(mla_norm_kv),
        rope=jnp.concatenate([cos, cos, -sin, sin], axis=1),
        wa=jnp.concatenate([jnp.concatenate([gla_wa_fwd, zeros], axis=1),
                            jnp.concatenate([zeros, gla_wa_bwd], axis=1)], axis=0).astype(BF16),
        ba=jnp.concatenate([row(gla_ba_fwd), row(gla_ba_bwd)], axis=1), gla_norm=row(gla_norm),
        wq_mla=jnp.transpose(wq_mla, (1, 0, 2)).astype(BF16),
        wkv_mla=jnp.transpose(wkv_mla, (1, 0, 2)).astype(BF16),
        w_o_gla=w_o_gla.astype(BF16), w_o_mla=w_o_mla.astype(BF16), w_out=w_out.astype(BF16),
        w_gate=w_gate.astype(BF16), w_up=w_up.astype(BF16), w_down=w_down.astype(BF16),
        g_post=row(norm_mix_post), g_ffn_pre=row(norm_ffn_pre), g_ffn_post=row(norm_ffn_post),
    )


def _layer(x, p):
    q, k, v, g, la_f, la_b, cq, ckv, kpe, ga, gb = _in_proj(x, p)
    oa = _gla(q, k, v, g, la_f, la_b, p)
    ob = _mla(cq, ckv, kpe, p)
    return _out(x, oa, ob, ga, gb, p)


def kernel(x_prompt, x_sample, norm_mix_pre, w_in, gla_wa_fwd, gla_ba_fwd, gla_wa_bwd, gla_ba_bwd, gla_norm, w_o_gla, mla_norm_q, w_uq, mla_norm_kv, w_uk, w_uv, w_o_mla, w_out, norm_mix_post, norm_ffn_pre, w_gate, w_up, w_down, norm_ffn_post):
    assert x_prompt.shape[1] == x_sample.shape[1]
    stacked = (norm_mix_pre, w_in, gla_wa_fwd, gla_ba_fwd, gla_wa_bwd, gla_ba_bwd, gla_norm,
               w_o_gla, mla_norm_q, w_uq, mla_norm_kv, w_uk, w_uv, w_o_mla, w_out,
               norm_mix_post, norm_ffn_pre, w_gate, w_up, w_down, norm_ffn_post)
    y_prompt, y_sample = x_prompt, x_sample
    for layer in range(norm_mix_pre.shape[0]):
        p = _prepare(x_prompt.shape[1], *[a[layer] for a in stacked])
        y_prompt, y_sample = _layer(y_prompt, p), _layer(y_sample, p)
    return (y_prompt, y_sample)
```
